```python
import jax
import jax.numpy as jnp
from jax import lax
import numpy as np


D_MODEL = 2048
BATCH = 8
SEQ = 2048
DEPTH = 4

A_HEADS = 8
A_HEAD_DIM = 128
A_WIDTH = A_HEADS * A_HEAD_DIM
MOBA_BLOCK = 256
MOBA_TOPK = 3
MOBA_Q_CHUNK = 8
B_HEADS = 16
B_HEAD_DIM = 64
B_WIDTH = B_HEADS * B_HEAD_DIM
LORA_W = 64
LORA_A = 64
LORA_G = 160
LORA_V = 32
C_WIDTH = D_MODEL
CONV_WIDTH = 3
D_FF = 5632
RMS_EPS = 1e-6
GN_EPS = 64e-5
NEG_INF = -1e30

kernel_name = 'moba_rwkv7_shortconv_hybrid'


def rms_norm(x, gain):
    xf = x.astype(jnp.float32)
    y = xf * lax.rsqrt(jnp.mean(xf * xf, axis=-1, keepdims=True) + RMS_EPS)
    return y.astype(x.dtype) * gain


def causal_dwconv(u, w):
    seq = u.shape[1]
    u_pad = jnp.pad(u, ((0, 0), (CONV_WIDTH - 1, 0), (0, 0)))
    out = w[0] * u_pad[:, :seq]
    for j in range(1, CONV_WIDTH):
        out = out + w[j] * u_pad[:, j:j + seq]
    return out


def alibi_slopes(n_heads):
    return jnp.exp2(-8.0 * jnp.arange(1, n_heads + 1, dtype=jnp.float32) / n_heads)


def moba_attention(q, k, v, q_gain, k_gain):
    bsz, seq, nh, hd = q.shape
    qn = rms_norm(q, q_gain).astype(jnp.float32).transpose(0, 2, 1, 3)
    kn = rms_norm(k, k_gain).astype(jnp.float32).transpose(0, 2, 1, 3)
    vh = v.astype(jnp.float32).transpose(0, 2, 1, 3)
    nb = -(-seq // MOBA_BLOCK)
    pad = nb * MOBA_BLOCK - seq
    kb = jnp.pad(kn, ((0, 0), (0, 0), (0, pad), (0, 0))).reshape(bsz, nh, nb, MOBA_BLOCK, hd)
    vb = jnp.pad(vh, ((0, 0), (0, 0), (0, pad), (0, 0))).reshape(bsz, nh, nb, MOBA_BLOCK, hd)
    k_mean = jnp.mean(kb, axis=3)
    n_sel = min(MOBA_TOPK, nb)
    slopes = alibi_slopes(nh)
    scale = hd ** -0.5
    b_ix = jnp.arange(bsz)[:, None, None, None]
    h_ix = jnp.arange(nh)[None, :, None, None]
    key_off = jnp.arange(MOBA_BLOCK)
    blk_ids = jnp.arange(nb)

    def chunk(c):
        t0 = c * MOBA_Q_CHUNK
        blk = t0 // MOBA_BLOCK
        qc = lax.dynamic_slice_in_dim(qn, t0, MOBA_Q_CHUNK, axis=2)
        t_pos = t0 + jnp.arange(MOBA_Q_CHUNK)
        gate = jnp.einsum('bhqd,bhnd->bhqn', qc, k_mean)
        gate = jnp.where(blk_ids < blk, gate, NEG_INF)
        _, sel = lax.top_k(gate, n_sel)
        valid = sel < blk
        k_sel = kb[b_ix, h_ix, sel]
        v_sel = vb[b_ix, h_ix, sel]
        s_past = jnp.einsum('bhqd,bhqjsd->bhqjs', qc, k_sel) * scale
        dist_past = (t_pos[:, None, None] - (sel[..., None] * MOBA_BLOCK + key_off)).astype(jnp.float32)
        s_past = jnp.where(valid[..., None], s_past - slopes[:, None, None, None] * dist_past, NEG_INF)
        k_own = lax.dynamic_index_in_dim(kb, blk, axis=2, keepdims=False)
        v_own = lax.dynamic_index_in_dim(vb, blk, axis=2, keepdims=False)
        s_own = jnp.einsum('bhqd,bhsd->bhqs', qc, k_own) * scale
        dist_own = (t_pos[:, None] - (blk * MOBA_BLOCK + key_off)[None, :]).astype(jnp.float32)
        s_own = jnp.where(dist_own >= 0, s_own - slopes[:, None, None] * dist_own, NEG_INF)
        scores = jnp.concatenate([s_past.reshape(bsz, nh, MOBA_Q_CHUNK, n_sel * MOBA_BLOCK), s_own], axis=-1)
        probs = jax.nn.softmax(scores, axis=-1)
        p_past = probs[..., :n_sel * MOBA_BLOCK].reshape(bsz, nh, MOBA_Q_CHUNK, n_sel, MOBA_BLOCK)
        p_own = probs[..., n_sel * MOBA_BLOCK:]
        return (jnp.einsum('bhqjs,bhqjsd->bhqd', p_past, v_sel)
                + jnp.einsum('bhqs,bhsd->bhqd', p_own, v_own))

    out = lax.map(chunk, jnp.arange(seq // MOBA_Q_CHUNK))
    out = out.transpose(1, 0, 3, 2, 4).reshape(bsz, seq, nh * hd)
    return out.astype(q.dtype)


def rwkv7_step(state, inp):
    r, w, k, v, a, b = inp
    sa = jnp.einsum('bhvk,bhk->bhv', state, a)
    state = state * w[:, :, None, :] + sa[..., None] * b[:, :, None, :] + v[..., None] * k[:, :, None, :]
    y = jnp.einsum('bhvk,bhk->bhv', state, r)
    return state, y


def rwkv7_time_mix(p, shift_mu, w0, w_lora, a0, a_lora, g_lora, k_k, k_a, r_k, gn_w, gn_b,
                   v0, v_lora, v_first):
    bsz, seq, _ = p.shape
    p_prev = jnp.pad(p, ((0, 0), (1, 0), (0, 0)))[:, :seq]
    p = p + shift_mu * (p_prev - p)
    r = p[..., :B_WIDTH]
    k = p[..., B_WIDTH:2 * B_WIDTH]
    v = p[..., 2 * B_WIDTH:3 * B_WIDTH]
    o = 3 * B_WIDTH
    z_w = p[..., o:o + LORA_W]
    o = o + LORA_W
    z_a = p[..., o:o + LORA_A]
    o = o + LORA_A
    z_g = p[..., o:o + LORA_G]
    o = o + LORA_G
    w = -jax.nn.softplus(-(w0 + jnp.tanh(z_w) @ w_lora)) - 0.5
    a = jax.nn.sigmoid(a0 + z_a @ a_lora)
    g = jax.nn.sigmoid(z_g) @ g_lora
    if v_lora is None:
        v_first = v
    else:
        z_v = p[..., o:o + LORA_V]
        v = v + (v_first - v) * jax.nn.sigmoid(v0 + z_v @ v_lora)

    def heads(t):
        return t.reshape(bsz, seq, B_HEADS, B_HEAD_DIM).astype(jnp.float32)

    kk = heads(k * k_k)
    kk = kk / jnp.maximum(jnp.sqrt(jnp.sum(kk * kk, axis=-1, keepdims=True)), 1e-12)
    k = k * (1.0 + (a - 1.0) * k_a)
    rh, kh, vh, ah = heads(r), heads(k), heads(v), heads(a)
    decay = jnp.exp(-jnp.exp(heads(w)))
    xs = tuple(jnp.moveaxis(t, 1, 0) for t in (rh, decay, kh, vh, -kk, kk * ah))
    state0 = jnp.zeros((bsz, B_HEADS, B_HEAD_DIM, B_HEAD_DIM), jnp.float32)
    _, y = lax.scan(rwkv7_step, state0, xs)
    y = jnp.moveaxis(y, 0, 1)
    mean = jnp.mean(y, axis=-1, keepdims=True)
    var = jnp.mean(jnp.square(y - mean), axis=-1, keepdims=True)
    y = ((y - mean) * lax.rsqrt(var + GN_EPS)).reshape(bsz, seq, B_WIDTH) * gn_w + gn_b
    bonus = jnp.sum(rh * kh * r_k, axis=-1, keepdims=True) * vh
    y = (y + bonus.reshape(bsz, seq, B_WIDTH)) * g
    return y.astype(p.dtype), v_first


def moba_rwkv_layer(x, norm_mix, w_in, q_gain, k_gain, shift_mu, w0, w_lora, a0, a_lora, g_lora,
                    k_k, k_a, r_k, gn_w, gn_b, w_out, v0, v_lora, v_first):
    bsz, seq, _ = x.shape
    proj = rms_norm(x, norm_mix) @ w_in
    qkv = proj[..., :3 * A_WIDTH].reshape(bsz, seq, 3, A_HEADS, A_HEAD_DIM)
    y_a = moba_attention(qkv[:, :, 0], qkv[:, :, 1], qkv[:, :, 2], q_gain, k_gain)
    y_b, v_first = rwkv7_time_mix(proj[..., 3 * A_WIDTH:], shift_mu, w0, w_lora, a0, a_lora, g_lora,
                                  k_k, k_a, r_k, gn_w, gn_b, v0, v_lora, v_first)
    y = jnp.concatenate([y_a, y_b], axis=-1) @ w_out
    return x + y, v_first


def short_conv_layer(x, norm_mix, conv_in, conv_w, conv_out):
    h = rms_norm(x, norm_mix) @ conv_in
    gate_b, gate_c, u = jnp.split(h, 3, axis=-1)
    return x + (gate_b * causal_dwconv(gate_c * u, conv_w)) @ conv_out


def conv_ffn(x, norm_ffn, ffn_up, ffn_conv, ffn_down):
    h = causal_dwconv(rms_norm(x, norm_ffn) @ ffn_up, ffn_conv)
    gate, val = jnp.split(h, 2, axis=-1)
    return x + (jax.nn.silu(gate) * val) @ ffn_down


def setup_inputs(seed: int = 0) -> dict:
    key = jax.random.key(seed)
    keys = list(jax.random.split(key, 128))

    def nrm(shape, scale):
        return scale * jax.random.normal(keys.pop(), shape, jnp.float32)

    def gain(shape):
        return 1.0 + nrm(shape, 0.02)

    def unif(shape, lo, hi):
        return jax.random.uniform(keys.pop(), shape, jnp.float32, lo, hi)

    inputs = {'x': nrm((BATCH, SEQ, D_MODEL), 1.0)}
    for li in range(DEPTH):
        p = 'l%d_' % li
        inputs[p + 'norm_mix'] = gain((D_MODEL,))
        if li % 2 == 0:
            first = li == 0
            lora_cols = LORA_W + LORA_A + LORA_G + (0 if first else LORA_V)
            rwkv_cols = 3 * B_WIDTH + lora_cols
            inputs[p + 'w_in'] = nrm((D_MODEL, 3 * A_WIDTH + rwkv_cols), D_MODEL ** -0.5)
            inputs[p + 'q_gain'] = gain((A_HEAD_DIM,))
            inputs[p + 'k_gain'] = gain((A_HEAD_DIM,))
            inputs[p + 'shift_mu'] = unif((rwkv_cols,), 0.2, 0.8)
            inputs[p + 'w0'] = unif((B_WIDTH,), -6.0, -1.0)
            inputs[p + 'w_lora'] = nrm((LORA_W, B_WIDTH), 0.5 * LORA_W ** -0.5)
            inputs[p + 'a0'] = nrm((B_WIDTH,), 0.1)
            inputs[p + 'a_lora'] = nrm((LORA_A, B_WIDTH), LORA_A ** -0.5)
            inputs[p + 'g_lora'] = nrm((LORA_G, B_WIDTH), LORA_G ** -0.5)
            inputs[p + 'k_k'] = 0.85 + nrm((B_WIDTH,), 0.05)
            inputs[p + 'k_a'] = 1.0 + nrm((B_WIDTH,), 0.05)
            inputs[p + 'r_k'] = nrm((B_HEADS, B_HEAD_DIM), 0.1)
            inputs[p + 'gn_w'] = gain((B_WIDTH,))
            inputs[p + 'gn_b'] = nrm((B_WIDTH,), 0.02)
            inputs[p + 'w_out'] = nrm((A_WIDTH + B_WIDTH, D_MODEL), (A_WIDTH + B_WIDTH) ** -0.5)
            if not first:
                inputs[p + 'v0'] = nrm((B_WIDTH,), 0.1)
                inputs[p + 'v_lora'] = nrm((LORA_V, B_WIDTH), LORA_V ** -0.5)
        else:
            inputs[p + 'conv_in'] = nrm((D_MODEL, 3 * C_WIDTH), D_MODEL ** -0.5)
            inputs[p + 'conv_w'] = nrm((CONV_WIDTH, C_WIDTH), 0.5)
            inputs[p + 'conv_out'] = nrm((C_WIDTH, D_MODEL), C_WIDTH ** -0.5)
        inputs[p + 'norm_ffn'] = gain((D_MODEL,))
        inputs[p + 'ffn_up'] = nrm((D_MODEL, 2 * D_FF), D_MODEL ** -0.5)
        inputs[p + 'ffn_conv'] = nrm((CONV_WIDTH, 2 * D_FF), 0.5)
        inputs[p + 'ffn_down'] = nrm((D_FF, D_MODEL), D_FF ** -0.5)
    return inputs


def reference(x,
              l0_norm_mix, l0_w_in, l0_q_gain, l0_k_gain, l0_shift_mu, l0_w0, l0_w_lora, l0_a0,
              l0_a_lora, l0_g_lora, l0_k_k, l0_k_a, l0_r_k, l0_gn_w, l0_gn_b, l0_w_out,
              l0_norm_ffn, l0_ffn_up, l0_ffn_conv, l0_ffn_down,
              l1_norm_mix, l1_conv_in, l1_conv_w, l1_conv_out,
              l1_norm_ffn, l1_ffn_up, l1_ffn_conv, l1_ffn_down,
              l2_norm_mix, l2_w_in, l2_q_gain, l2_k_gain, l2_shift_mu, l2_w0, l2_w_lora, l2_a0,
              l2_a_lora, l2_g_lora, l2_k_k, l2_k_a, l2_r_k, l2_gn_w, l2_gn_b, l2_w_out,
              l2_v0, l2_v_lora,
              l2_norm_ffn, l2_ffn_up, l2_ffn_conv, l2_ffn_down,
              l3_norm_mix, l3_conv_in, l3_conv_w, l3_conv_out,
              l3_norm_ffn, l3_ffn_up, l3_ffn_conv, l3_ffn_down):
    mix_layers = (
        (l0_norm_mix, l0_w_in, l0_q_gain, l0_k_gain, l0_shift_mu, l0_w0, l0_w_lora, l0_a0,
         l0_a_lora, l0_g_lora, l0_k_k, l0_k_a, l0_r_k, l0_gn_w, l0_gn_b, l0_w_out, None, None),
        (l1_norm_mix, l1_conv_in, l1_conv_w, l1_conv_out),
        (l2_norm_mix, l2_w_in, l2_q_gain, l2_k_gain, l2_shift_mu, l2_w0, l2_w_lora, l2_a0,
         l2_a_lora, l2_g_lora, l2_k_k, l2_k_a, l2_r_k, l2_gn_w, l2_gn_b, l2_w_out, l2_v0, l2_v_lora),
        (l3_norm_mix, l3_conv_in, l3_conv_w, l3_conv_out),
    )
    ffn_layers = (
        (l0_norm_ffn, l0_ffn_up, l0_ffn_conv, l0_ffn_down),
        (l1_norm_ffn, l1_ffn_up, l1_ffn_conv, l1_ffn_down),
        (l2_norm_ffn, l2_ffn_up, l2_ffn_conv, l2_ffn_down),
        (l3_norm_ffn, l3_ffn_up, l3_ffn_conv, l3_ffn_down),
    )
    v_first = None
    for li in range(DEPTH):
        if li % 2 == 0:
            x, v_first = moba_rwkv_layer(x, *mix_layers[li], v_first)
        else:
            x = short_conv_layer(x, *mix_layers[li])
        x = conv_ffn(x, *ffn_layers[li])
    return x
```

```python
import functools

import jax
import jax.numpy as jnp
from jax import lax
from jax.experimental import pallas as pl
from jax.experimental.pallas import tpu as pltpu

F32 = jnp.float32
BF16 = jnp.bfloat16

A_HEADS = 8
A_HEAD_DIM = 128
A_WIDTH = A_HEADS * A_HEAD_DIM
MOBA_BLOCK = 256
MOBA_TOPK = 3
B_HEADS = 16
B_HEAD_DIM = 64
B_WIDTH = B_HEADS * B_HEAD_DIM
LORA_W = 64
LORA_A = 64
LORA_G = 160
LORA_V = 32
CONV_WIDTH = 3
RMS_EPS = 1e-6
GN_EPS = 64e-5
NEG_INF = -1e30

V7X_LANES = 128
V7X_SUBLANES = 8
V7X_VMEM_BYTES = 64 * 1024 * 1024
V7X_VMEM_CAP = V7X_VMEM_BYTES - 8 * 1024 * 1024
SPILL_ALLOWANCE = 4 * 1024 * 1024

ZW_OFF, ZW_PAD = 0, 128
ZA_OFF, ZA_PAD = 128, 128
ZG_OFF, ZG_PAD = 256, 256
ZV_OFF, ZV_PAD = 512, 128
Z_WIDTH = 640


def _vmem_limit(nbytes):
    return int(min(max(nbytes + SPILL_ALLOWANCE, 16 * 1024 * 1024), V7X_VMEM_CAP))


def _nbytes(shape, dtype):
    n = 1
    for s in shape:
        n *= s
    return n * jnp.dtype(dtype).itemsize


def _pick(n, prefs):
    for p in prefs:
        if n % p == 0:
            return p
    return n


def _norm_matmul_body(x_ref, g_ref, w_ref, o_ref, xn_ref):
    @pl.when(pl.program_id(1) == 0)
    def _():
        x = x_ref[...]
        ms = jnp.mean(x * x, axis=-1, keepdims=True)
        xn_ref[...] = (x * lax.rsqrt(ms + RMS_EPS) * g_ref[...]).astype(BF16)

    o_ref[...] = jnp.dot(xn_ref[...], w_ref[...], preferred_element_type=F32).astype(o_ref.dtype)


def norm_matmul(x, gain, w, out_dtype, name):
    m, k = x.shape
    n = w.shape[1]
    tm = _pick(m, (1024, 512, 256, 128))
    tn = _pick(n, (1024, 768, 640, 512, 256, 128))
    est = (2 * _nbytes((tm, k), F32) + _nbytes((tm, k), BF16) + 2 * _nbytes((k, tn), BF16)
           + 2 * _nbytes((tm, tn), out_dtype) + 2 * _nbytes((tm, tn), F32) + _nbytes((tm, k), F32))
    return pl.pallas_call(
        _norm_matmul_body,
        grid=(m // tm, n // tn),
        in_specs=[
            pl.BlockSpec((tm, k), lambda i, j: (i, 0)),
            pl.BlockSpec((1, k), lambda i, j: (0, 0)),
            pl.BlockSpec((k, tn), lambda i, j: (0, j)),
        ],
        out_specs=pl.BlockSpec((tm, tn), lambda i, j: (i, j)),
        out_shape=jax.ShapeDtypeStruct((m, n), out_dtype),
        scratch_shapes=[pltpu.VMEM((tm, k), BF16)],
        compiler_params=pltpu.CompilerParams(
            dimension_semantics=("parallel", "arbitrary"), vmem_limit_bytes=_vmem_limit(est)),
        name=name,
    )(x, gain.reshape(1, k), w)


def _matmul_res_body(a_ref, w_ref, r_ref, o_ref):
    o_ref[...] = r_ref[...] + jnp.dot(a_ref[...], w_ref[...], preferred_element_type=F32)


def matmul_residual(a, w, res, name):
    m, k = a.shape
    n = w.shape[1]
    tm = _pick(m, (1024, 512, 256, 128))
    tn = _pick(n, (512, 256, 128)) if k > 4096 else _pick(n, (1024, 512, 256, 128))
    est = (2 * _nbytes((tm, k), BF16) + 2 * _nbytes((k, tn), BF16) + 6 * _nbytes((tm, tn), F32))
    return pl.pallas_call(
        _matmul_res_body,
        grid=(m // tm, n // tn),
        in_specs=[
            pl.BlockSpec((tm, k), lambda i, j: (i, 0)),
            pl.BlockSpec((k, tn), lambda i, j: (0, j)),
            pl.BlockSpec((tm, tn), lambda i, j: (i, j)),
        ],
        out_specs=pl.BlockSpec((tm, tn), lambda i, j: (i, j)),
        out_shape=jax.ShapeDtypeStruct((m, n), F32),
        compiler_params=pltpu.CompilerParams(
            dimension_semantics=("parallel", "parallel"), vmem_limit_bytes=_vmem_limit(est)),
        name=name,
    )(a, w, res)


def _mix_out_body(ya_ref, yb_ref, g_ref, wa_ref, wb_ref, r_ref, o_ref):
    yb = (yb_ref[...] * g_ref[...]).astype(BF16)
    acc = jnp.dot(ya_ref[...], wa_ref[...], preferred_element_type=F32)
    acc = acc + jnp.dot(yb, wb_ref[...], preferred_element_type=F32)
    o_ref[...] = r_ref[...] + acc


def mix_out_proj(ya, yb, g, w_out, res, name):
    m, ka = ya.shape
    kb = yb.shape[1]
    n = w_out.shape[1]
    tm = _pick(m, (512, 256, 128))
    tn = _pick(n, (1024, 512, 256, 128))
    est = (2 * _nbytes((tm, ka), BF16) + 4 * _nbytes((tm, kb), F32) + 2 * _nbytes((ka + kb, tn), BF16)
           + 6 * _nbytes((tm, tn), F32) + 2 * _nbytes((tm, kb), F32))
    return pl.pallas_call(
        _mix_out_body,
        grid=(m // tm, n // tn),
        in_specs=[
            pl.BlockSpec((tm, ka), lambda i, j: (i, 0)),
            pl.BlockSpec((tm, kb), lambda i, j: (i, 0)),
            pl.BlockSpec((tm, kb), lambda i, j: (i, 0)),
            pl.BlockSpec((ka, tn), lambda i, j: (0, j)),
            pl.BlockSpec((kb, tn), lambda i, j: (0, j)),
            pl.BlockSpec((tm, tn), lambda i, j: (i, j)),
        ],
        out_specs=pl.BlockSpec((tm, tn), lambda i, j: (i, j)),
        out_shape=jax.ShapeDtypeStruct((m, n), F32),
        compiler_params=pltpu.CompilerParams(
            dimension_semantics=("parallel", "parallel"), vmem_limit_bytes=_vmem_limit(est)),
        name=name,
    )(ya, yb, g, w_out[:ka], w_out[ka:], res)


def _shift_rows(u, s):
    rolled = pltpu.roll(u, s, 0)
    row = lax.broadcasted_iota(jnp.int32, u.shape, 0)
    return jnp.where(row >= s, rolled, 0.0)


def _causal_conv3(u, w):
    out = w[0:1, :] * _shift_rows(u, 2)
    out = out + w[1:2, :] * _shift_rows(u, 1)
    return out + w[2:3, :] * u


def _ffn_conv_body(hg_ref, hv_ref, wg_ref, wv_ref, o_ref):
    gate = _causal_conv3(hg_ref[0].astype(F32), wg_ref[...])
    val = _causal_conv3(hv_ref[0].astype(F32), wv_ref[...])
    o_ref[0] = (jax.nn.silu(gate) * val).astype(o_ref.dtype)


def ffn_conv_gate(h, conv_w, name):
    b, s, f2 = h.shape
    f = f2 // 2
    tc = _pick(f, (256, 128))
    nj = f // tc
    est = 4 * _nbytes((s, tc), h.dtype) + 2 * _nbytes((s, tc), BF16) + 10 * _nbytes((s, tc), F32)
    return pl.pallas_call(
        _ffn_conv_body,
        grid=(b, nj),
        in_specs=[
            pl.BlockSpec((1, s, tc), lambda i, j: (i, 0, j)),
            pl.BlockSpec((1, s, tc), lambda i, j: (i, 0, nj + j)),
            pl.BlockSpec((CONV_WIDTH, tc), lambda i, j: (0, j)),
            pl.BlockSpec((CONV_WIDTH, tc), lambda i, j: (0, nj + j)),
        ],
        out_specs=pl.BlockSpec((1, s, tc), lambda i, j: (i, 0, j)),
        out_shape=jax.ShapeDtypeStruct((b, s, f), BF16),
        compiler_params=pltpu.CompilerParams(
            dimension_semantics=("parallel", "parallel"), vmem_limit_bytes=_vmem_limit(est)),
        name=name,
    )(h, h, conv_w, conv_w)


def _short_conv_body(hb_ref, hc_ref, hu_ref, w_ref, o_ref):
    cu = hc_ref[0].astype(F32) * hu_ref[0].astype(F32)
    o_ref[0] = (hb_ref[0].astype(F32) * _causal_conv3(cu, w_ref[...])).astype(o_ref.dtype)


def short_conv_gate(h, conv_w, name):
    b, s, c3 = h.shape
    c = c3 // 3
    tc = _pick(c, (256, 128))
    nj = c // tc
    est = 6 * _nbytes((s, tc), h.dtype) + 2 * _nbytes((s, tc), BF16) + 8 * _nbytes((s, tc), F32)
    return pl.pallas_call(
        _short_conv_body,
        grid=(b, nj),
        in_specs=[
            pl.BlockSpec((1, s, tc), lambda i, j: (i, 0, j)),
            pl.BlockSpec((1, s, tc), lambda i, j: (i, 0, nj + j)),
            pl.BlockSpec((1, s, tc), lambda i, j: (i, 0, 2 * nj + j)),
            pl.BlockSpec((CONV_WIDTH, tc), lambda i, j: (0, j)),
        ],
        out_specs=pl.BlockSpec((1, s, tc), lambda i, j: (i, 0, j)),
        out_shape=jax.ShapeDtypeStruct((b, s, c), BF16),
        compiler_params=pltpu.CompilerParams(
            dimension_semantics=("parallel", "parallel"), vmem_limit_bytes=_vmem_limit(est)),
        name=name,
    )(h, h, h, conv_w)


def _head_rms(x, gain):
    return x * lax.rsqrt(jnp.mean(x * x, axis=-1, keepdims=True) + RMS_EPS) * gain


def _moba_body(q_ref, k_ref, v_ref, qg_ref, kg_ref, o_ref, kn_ref, vb_ref, km_ref, *, n_heads, n_blocks):
    bs = MOBA_BLOCK
    hd = A_HEAD_DIM
    h = pl.program_id(1)
    i = pl.program_id(2)

    @pl.when(i == 0)
    def _():
        km_ref[...] = jnp.zeros_like(km_ref)
        for nb in range(n_blocks):
            kn = _head_rms(k_ref[0, nb * bs:(nb + 1) * bs, :], kg_ref[...])
            kn_ref[nb * bs:(nb + 1) * bs, :] = kn.astype(BF16)
            km_ref[nb:nb + 1, :] = jnp.mean(kn, axis=0, keepdims=True)
        vb_ref[...] = v_ref[0].astype(BF16)

    qn = _head_rms(q_ref[0], qg_ref[...])
    qb = qn.astype(BF16)
    scale = hd ** -0.5
    slope = jnp.exp2(jnp.full((1, bs), -8.0 / n_heads, F32) * (h + 1).astype(F32))
    rel = (lax.broadcasted_iota(jnp.int32, (bs, bs), 0)
           - lax.broadcasted_iota(jnp.int32, (bs, bs), 1)).astype(F32)

    gate = lax.dot_general(qn, km_ref[...], (((1,), (1,)), ((), ())),
                           precision=lax.Precision.HIGHEST, preferred_element_type=F32)
    lane = lax.broadcasted_iota(jnp.int32, gate.shape, 1)
    gate = jnp.where(lane < i, gate, NEG_INF)
    in_range = lane < n_blocks

    def attend(kb, vb, s_bias_mask):
        s = lax.dot_general(qb, kb, (((1,), (1,)), ((), ())), preferred_element_type=F32) * scale
        return s_bias_mask(s)

    row0 = pl.multiple_of(i * bs, bs)
    s_own = attend(kn_ref[pl.ds(row0, bs), :], None,
                   lambda s: jnp.where(rel >= 0, s - slope * rel, NEG_INF))
    m0 = jnp.max(s_own, axis=-1, keepdims=True)
    p0 = jnp.exp(s_own - m0)
    l0 = jnp.sum(p0, axis=-1, keepdims=True)
    acc0 = jnp.dot(p0.astype(BF16), vb_ref[pl.ds(row0, bs), :], preferred_element_type=F32)

    def past_block(n, carry):
        m_run, l_run, acc = carry
        g_n = jnp.sum(jnp.where(lane == n, gate, 0.0), axis=-1, keepdims=True)
        beats = jnp.where((gate > g_n) | ((gate == g_n) & (lane < n)), 1.0, 0.0)
        cnt = jnp.sum(jnp.where(in_range, beats, 0.0), axis=-1, keepdims=True)
        chosen = cnt < float(MOBA_TOPK)
        rown = pl.multiple_of(n * bs, bs)
        dist = rel + ((i - n) * bs).astype(F32)
        s = attend(kn_ref[pl.ds(rown, bs), :], None,
                   lambda s_: jnp.where(chosen, s_ - slope * dist, NEG_INF))
        m_new = jnp.maximum(m_run, jnp.max(s, axis=-1, keepdims=True))
        alpha = jnp.exp(m_run - m_new)
        p = jnp.exp(s - m_new)
        l_new = alpha * l_run + jnp.sum(p, axis=-1, keepdims=True)
        acc_new = alpha * acc + jnp.dot(p.astype(BF16), vb_ref[pl.ds(rown, bs), :],
                                        preferred_element_type=F32)
        return m_new, l_new, acc_new

    _, l_fin, acc_fin = lax.fori_loop(0, i, past_block, (m0, l0, acc0))
    o_ref[0] = (acc_fin / l_fin).astype(o_ref.dtype)


def moba_attention(proj, q_gain, k_gain, name):
    b, s, _ = proj.shape
    assert s % MOBA_BLOCK == 0
    nb = s // MOBA_BLOCK
    assert nb <= V7X_LANES
    hd = A_HEAD_DIM
    est = (2 * _nbytes((MOBA_BLOCK, hd), F32) + 4 * _nbytes((s, hd), F32) + 2 * _nbytes((s, hd), BF16)
           + 2 * _nbytes((MOBA_BLOCK, hd), BF16) + 24 * _nbytes((MOBA_BLOCK, MOBA_BLOCK), F32)
           + 2 * _nbytes((s, hd), F32))
    return pl.pallas_call(
        functools.partial(_moba_body, n_heads=A_HEADS, n_blocks=nb),
        grid=(b, A_HEADS, nb),
        in_specs=[
            pl.BlockSpec((1, MOBA_BLOCK, hd), lambda bi, h, i: (bi, i, h)),
            pl.BlockSpec((1, s, hd), lambda bi, h, i: (bi, 0, A_HEADS + h)),
            pl.BlockSpec((1, s, hd), lambda bi, h, i: (bi, 0, 2 * A_HEADS + h)),
            pl.BlockSpec((1, hd), lambda bi, h, i: (0, 0)),
            pl.BlockSpec((1, hd), lambda bi, h, i: (0, 0)),
        ],
        out_specs=pl.BlockSpec((1, MOBA_BLOCK, hd), lambda bi, h, i: (bi, i, h)),
        out_shape=jax.ShapeDtypeStruct((b, s, A_WIDTH), BF16),
        scratch_shapes=[
            pltpu.VMEM((s, hd), BF16),
            pltpu.VMEM((s, hd), BF16),
            pltpu.VMEM((V7X_LANES, hd), F32),
        ],
        compiler_params=pltpu.CompilerParams(
            dimension_semantics=("parallel", "parallel", "arbitrary"), vmem_limit_bytes=_vmem_limit(est)),
        name=name,
    )(proj, proj, proj, q_gain.reshape(1, hd), k_gain.reshape(1, hd))


def _token_shift(p, prev_last, mu):
    rolled = pltpu.roll(p, 1, 0)
    row = lax.broadcasted_iota(jnp.int32, p.shape, 0)
    p_prev = jnp.where(row == 0, prev_last, rolled)
    return p + mu * (p_prev - p)


def _rwkv_prep_body(*refs, has_vres):
    if has_vres:
        (pm_ref, pz_ref, mum_ref, muz_ref, w0_ref, wl_ref, a0_ref, al_ref, gl_ref, v0_ref, vl_ref, vf_ref,
         r_ref, d_ref, k_ref, v_ref, a_ref, g_ref, cm_ref, cz_ref) = refs
    else:
        (pm_ref, pz_ref, mum_ref, muz_ref, w0_ref, wl_ref, a0_ref, al_ref, gl_ref,
         r_ref, d_ref, k_ref, v_ref, a_ref, g_ref, cm_ref, cz_ref) = refs
    tm = pm_ref.shape[1]
    bw = B_WIDTH

    @pl.when(pl.program_id(1) == 0)
    def _():
        cm_ref[...] = jnp.zeros_like(cm_ref)
        cz_ref[...] = jnp.zeros_like(cz_ref)

    pm = pm_ref[0]
    pz = pz_ref[0]
    last = V7X_SUBLANES - 1
    pms = _token_shift(pm, cm_ref[last:last + 1, :], mum_ref[...])
    pzs = _token_shift(pz, cz_ref[last:last + 1, :], muz_ref[...])
    cm_ref[...] = pm[tm - V7X_SUBLANES:, :]
    cz_ref[...] = pz[tm - V7X_SUBLANES:, :]

    r_ref[0] = pms[:, :bw]
    k_ref[0] = pms[:, bw:2 * bw]
    v = pms[:, 2 * bw:3 * bw]

    z_w = jnp.tanh(pzs[:, ZW_OFF:ZW_OFF + ZW_PAD]).astype(BF16)
    w = w0_ref[...] + jnp.dot(z_w, wl_ref[...], preferred_element_type=F32)
    w = -jax.nn.softplus(-w) - 0.5
    d_ref[0] = jnp.exp(-jnp.exp(w))

    z_a = pzs[:, ZA_OFF:ZA_OFF + ZA_PAD].astype(BF16)
    a_ref[0] = jax.nn.sigmoid(a0_ref[...] + jnp.dot(z_a, al_ref[...], preferred_element_type=F32))

    z_g = jax.nn.sigmoid(pzs[:, ZG_OFF:ZG_OFF + ZG_PAD]).astype(BF16)
    g_ref[0] = jnp.dot(z_g, gl_ref[...], preferred_element_type=F32)

    if has_vres:
        z_v = pzs[:, ZV_OFF:ZV_OFF + ZV_PAD].astype(BF16)
        mix = jax.nn.sigmoid(v0_ref[...] + jnp.dot(z_v, vl_ref[...], preferred_element_type=F32))
        v = v + (vf_ref[0] - v) * mix
    v_ref[0] = v


def _pad_rows(w, rows):
    return jnp.pad(w, ((0, rows - w.shape[0]), (0, 0)))


def rwkv_prep(proj, pz, shift_mu, w0, w_lora, a0, a_lora, g_lora, v0, v_lora, v_first, name):
    b, s, _ = proj.shape
    bw = B_WIDTH
    has_vres = v_lora is not None
    tm = _pick(s, (256, 128))
    main_blk = 3 * bw
    assert (3 * A_WIDTH) % main_blk == 0

    mu_main = shift_mu[:3 * bw].reshape(1, 3 * bw)
    lora_mu = shift_mu[3 * bw:]
    mu_z = jnp.zeros((Z_WIDTH,), F32)
    mu_z = mu_z.at[ZW_OFF:ZW_OFF + LORA_W].set(lora_mu[:LORA_W])
    mu_z = mu_z.at[ZA_OFF:ZA_OFF + LORA_A].set(lora_mu[LORA_W:LORA_W + LORA_A])
    mu_z = mu_z.at[ZG_OFF:ZG_OFF + LORA_G].set(lora_mu[LORA_W + LORA_A:LORA_W + LORA_A + LORA_G])
    if has_vres:
        mu_z = mu_z.at[ZV_OFF:ZV_OFF + LORA_V].set(lora_mu[LORA_W + LORA_A + LORA_G:])
    mu_z = mu_z.reshape(1, Z_WIDTH)

    row = lambda t: t.reshape(1, bw)
    const = lambda shape: pl.BlockSpec(shape, lambda bi, i: (0, 0))
    tok = pl.BlockSpec((1, tm, bw), lambda bi, i: (bi, i, 0))
    args = [proj, pz, mu_main, mu_z, row(w0), _pad_rows(w_lora, ZW_PAD).astype(BF16),
            row(a0), _pad_rows(a_lora, ZA_PAD).astype(BF16), _pad_rows(g_lora, ZG_PAD).astype(BF16)]
    in_specs = [
        pl.BlockSpec((1, tm, main_blk), lambda bi, i: (bi, i, (3 * A_WIDTH) // main_blk)),
        pl.BlockSpec((1, tm, Z_WIDTH), lambda bi, i: (bi, i, 0)),
        const((1, main_blk)), const((1, Z_WIDTH)),
        const((1, bw)), const((ZW_PAD, bw)), const((1, bw)), const((ZA_PAD, bw)), const((ZG_PAD, bw)),
    ]
    if has_vres:
        args += [row(v0), _pad_rows(v_lora, ZV_PAD).astype(BF16), v_first]
        in_specs += [const((1, bw)), const((ZV_PAD, bw)), tok]
    out_sds = jax.ShapeDtypeStruct((b, s, bw), F32)
    est = (2 * _nbytes((tm, main_blk + Z_WIDTH), F32) + 16 * _nbytes((tm, bw), F32)
           + 6 * _nbytes((tm, main_blk), F32) + 4 * _nbytes((ZG_PAD + 3 * ZW_PAD, bw), BF16))
    return pl.pallas_call(
        functools.partial(_rwkv_prep_body, has_vres=has_vres),
        grid=(b, s // tm),
        in_specs=in_specs,
        out_specs=[tok] * 6,
        out_shape=[out_sds] * 6,
        scratch_shapes=[pltpu.VMEM((V7X_SUBLANES, main_blk), F32), pltpu.VMEM((V7X_SUBLANES, Z_WIDTH), F32)],
        compiler_params=pltpu.CompilerParams(
            dimension_semantics=("parallel", "arbitrary"), vmem_limit_bytes=_vmem_limit(est)),
        name=name,
    )(*args)


def _rwkv_scan_body(r_ref, d_ref, k_ref, v_ref, a_ref, kk_ref, ka_ref, rk_ref, gw_ref, gb_ref,
                    o_ref, st_ref, y_ref):
    tc = r_ref.shape[0]
    nd = B_HEAD_DIM

    @pl.when(pl.program_id(0) == 0)
    def _():
        st_ref[...] = jnp.zeros_like(st_ref)

    def step(t, carry):
        r_t = r_ref[t]
        d_t = d_ref[t]
        k_t = k_ref[t]
        v_t = v_ref[t]
        a_t = a_ref[t]
        kk = k_t * kk_ref[...]
        nrm = jnp.sqrt(jnp.sum(kk * kk, axis=0, keepdims=True))
        kk = kk / jnp.maximum(nrm, 1e-12)
        k_m = k_t * (1.0 + (a_t - 1.0) * ka_ref[...])
        a_s = -kk
        b_s = kk * a_t
        for vi in range(nd):
            s_v = st_ref[vi]
            sa = jnp.sum(s_v * a_s, axis=0, keepdims=True)
            s_v = s_v * d_t + sa * b_s + v_t[vi:vi + 1, :] * k_m
            st_ref[vi] = s_v
            y_ref[vi:vi + 1, :] = jnp.sum(s_v * r_t, axis=0, keepdims=True)
        y = y_ref[...]
        mean = jnp.mean(y, axis=0, keepdims=True)
        var = jnp.mean(jnp.square(y - mean), axis=0, keepdims=True)
        y = (y - mean) * lax.rsqrt(var + GN_EPS) * gw_ref[...] + gb_ref[...]
        bonus = jnp.sum(r_t * k_m * rk_ref[...], axis=0, keepdims=True) * v_t
        o_ref[t] = y + bonus
        return carry

    lax.fori_loop(0, tc, step, 0)


def rwkv_scan(r, d, k, v, a, k_k, k_a, r_k, gn_w, gn_b, name):
    s, nd, n = r.shape
    tc = _pick(s, (64, 32, 16, 8))
    seq = pl.BlockSpec((tc, nd, n), lambda i: (i, 0, 0))
    par = pl.BlockSpec((nd, n), lambda i: (0, 0))
    est = 12 * _nbytes((tc, nd, n), F32) + _nbytes((nd, nd, n), F32) + 16 * _nbytes((nd, n), F32)
    return pl.pallas_call(
        _rwkv_scan_body,
        grid=(s // tc,),
        in_specs=[seq] * 5 + [par] * 5,
        out_specs=seq,
        out_shape=jax.ShapeDtypeStruct((s, nd, n), F32),
        scratch_shapes=[pltpu.VMEM((nd, nd, n), F32), pltpu.VMEM((nd, n), F32)],
        compiler_params=pltpu.CompilerParams(
            dimension_semantics=("arbitrary",), vmem_limit_bytes=_vmem_limit(est)),
        name=name,
    )(r, d, k, v, a, k_k, k_a, r_k, gn_w, gn_b)


def _to_scan_layout(t, b, s):
    return jnp.transpose(t.reshape(b, s, B_HEADS, B_HEAD_DIM), (1, 3, 0, 2)).reshape(s, B_HEAD_DIM, b * B_HEADS)


def _from_scan_layout(t, b, s):
    return jnp.transpose(t.reshape(s, B_HEAD_DIM, b, B_HEADS), (2, 0, 3, 1)).reshape(b * s, B_WIDTH)


def _param_tile(p, b):
    return jnp.tile(p.reshape(B_HEADS, B_HEAD_DIM).T, (1, b))


def _lora_proj_weight(w_in, has_vres):
    o = 3 * A_WIDTH + 3 * B_WIDTH
    d = w_in.shape[0]
    wz = jnp.zeros((d, Z_WIDTH), w_in.dtype)
    wz = wz.at[:, ZW_OFF:ZW_OFF + LORA_W].set(w_in[:, o:o + LORA_W])
    o += LORA_W
    wz = wz.at[:, ZA_OFF:ZA_OFF + LORA_A].set(w_in[:, o:o + LORA_A])
    o += LORA_A
    wz = wz.at[:, ZG_OFF:ZG_OFF + LORA_G].set(w_in[:, o:o + LORA_G])
    o += LORA_G
    if has_vres:
        wz = wz.at[:, ZV_OFF:ZV_OFF + LORA_V].set(w_in[:, o:o + LORA_V])
    return wz


def moba_rwkv_layer(x, b, s, li, norm_mix, w_in, q_gain, k_gain, shift_mu, w0, w_lora, a0, a_lora, g_lora,
                    k_k, k_a, r_k, gn_w, gn_b, w_out, v0, v_lora, v_first):
    has_vres = v_lora is not None
    tag = "l%d_" % li
    n_main = 3 * A_WIDTH + 3 * B_WIDTH
    proj = norm_matmul(x, norm_mix, w_in[:, :n_main].astype(BF16), F32, tag + "in_proj")
    pz = norm_matmul(x, norm_mix, _lora_proj_weight(w_in, has_vres).astype(BF16), F32, tag + "lora_proj")
    proj = proj.reshape(b, s, n_main)
    pz = pz.reshape(b, s, Z_WIDTH)

    y_a = moba_attention(proj, q_gain, k_gain, tag + "moba")

    r, d, k, v, a, g = rwkv_prep(proj, pz, shift_mu, w0, w_lora, a0, a_lora, g_lora, v0, v_lora, v_first,
                                 tag + "rwkv_prep")
    if not has_vres:
        v_first = v
    tiles = [_param_tile(p, b) for p in (k_k, k_a, r_k.reshape(-1), gn_w, gn_b)]
    y_b = rwkv_scan(*[_to_scan_layout(t, b, s) for t in (r, d, k, v, a)], *tiles, tag + "rwkv_scan")
    y_b = _from_scan_layout(y_b, b, s)

    x = mix_out_proj(y_a.reshape(b * s, A_WIDTH), y_b, g.reshape(b * s, B_WIDTH), w_out.astype(BF16), x,
                     tag + "out_proj")
    return x, v_first


def short_conv_layer(x, b, s, li, norm_mix, conv_in, conv_w, conv_out):
    tag = "l%d_" % li
    h = norm_matmul(x, norm_mix, conv_in.astype(BF16), BF16, tag + "conv_in")
    m = short_conv_gate(h.reshape(b, s, -1), conv_w, tag + "conv_gate")
    return matmul_residual(m.reshape(b * s, -1), conv_out.astype(BF16), x, tag + "conv_out")


def conv_ffn(x, b, s, li, norm_ffn, ffn_up, ffn_conv, ffn_down):
    tag = "l%d_" % li
    h = norm_matmul(x, norm_ffn, ffn_up.astype(BF16), BF16, tag + "ffn_up")
    m = ffn_conv_gate(h.reshape(b, s, -1), ffn_conv, tag + "ffn_gate")
    return matmul_residual(m.reshape(b * s, -1), ffn_down.astype(BF16), x, tag + "ffn_down")


def kernel(x, l0_norm_mix, l0_w_in, l0_q_gain, l0_k_gain, l0_shift_mu, l0_w0, l0_w_lora, l0_a0, l0_a_lora, l0_g_lora, l0_k_k, l0_k_a, l0_r_k, l0_gn_w, l0_gn_b, l0_w_out, l0_norm_ffn, l0_ffn_up, l0_ffn_conv, l0_ffn_down, l1_norm_mix, l1_conv_in, l1_conv_w, l1_conv_out, l1_norm_ffn, l1_ffn_up, l1_ffn_conv, l1_ffn_down, l2_norm_mix, l2_w_in, l2_q_gain, l2_k_gain, l2_shift_mu, l2_w0, l2_w_lora, l2_a0, l2_a_lora, l2_g_lora, l2_k_k, l2_k_a, l2_r_k, l2_gn_w, l2_gn_b, l2_w_out, l2_v0, l2_v_lora, l2_norm_ffn, l2_ffn_up, l2_ffn_conv, l2_ffn_down, l3_norm_mix, l3_conv_in, l3_conv_w, l3_conv_out, l3_norm_ffn, l3_ffn_up, l3_ffn_conv, l3_ffn_down):
    b, s, d = x.shape
    h = x.reshape(b * s, d)

    h, v_first = moba_rwkv_layer(
        h, b, s, 0, l0_norm_mix, l0_w_in, l0_q_gain, l0_k_gain, l0_shift_mu, l0_w0, l0_w_lora, l0_a0,
        l0_a_lora, l0_g_lora, l0_k_k, l0_k_a, l0_r_k, l0_gn_w, l0_gn_b, l0_w_out, None, None, None)
    h = conv_ffn(h, b, s, 0, l0_norm_ffn, l0_ffn_up, l0_ffn_conv, l0_ffn_down)

    h = short_conv_layer(h, b, s, 1, l1_norm_mix, l1_conv_in, l1_conv_w, l1_conv_out)
    h = conv_ffn(h, b, s, 1, l1_norm_ffn, l1_ffn_up, l1_ffn_conv, l1_ffn_down)

    h, _ = moba_rwkv_layer(
        h, b, s, 2, l2_norm_mix, l2_w_in, l2_q_gain, l2_k_gain, l2_shift_mu, l2_w0, l2_w_lora, l2_a0,
        l2_a_lora, l2_g_lora, l2_k_k, l2_k_a, l2_r_k, l2_gn_w, l2_gn_b, l2_w_out, l2_v0, l2_v_lora, v_first)
    h = conv_ffn(h, b, s, 2, l2_norm_ffn, l2_ffn_up, l2_ffn_conv, l2_ffn_down)

    h = short_conv_layer(h, b, s, 3, l3_norm_mix, l3_conv_in, l3_conv_w, l3_conv_out)
    h = conv_ffn(h, b, s, 3, l3_norm_ffn, l3_ffn_up, l3_ffn_conv, l3_ffn_down)
    return h.reshape(b, s, d)
```

```python
import functools

import jax
import jax.numpy as jnp
from jax import lax
from jax.experimental import pallas as pl
from jax.experimental.pallas import tpu as pltpu

F32 = jnp.float32
BF16 = jnp.bfloat16

A_HEADS = 8
A_HEAD_DIM = 128
A_WIDTH = A_HEADS * A_HEAD_DIM
MOBA_BLOCK = 256
MOBA_TOPK = 3
MOBA_HEADS_PER_STEP = 2
MOBA_DENOM_ROWS = 16
B_HEADS = 16
B_HEAD_DIM = 64
B_WIDTH = B_HEADS * B_HEAD_DIM
LORA_W = 64
LORA_A = 64
LORA_G = 160
LORA_V = 32
CONV_WIDTH = 3
RMS_EPS = 1e-6
GN_EPS = 64e-5
NEG_INF = -1e30
LOG2E = 1.4426950408889634

V7X_LANES = 128
V7X_SUBLANES = 8
V7X_VMEM_BYTES = 64 * 1024 * 1024
V7X_VMEM_CAP = V7X_VMEM_BYTES - 8 * 1024 * 1024
SPILL_ALLOWANCE = 4 * 1024 * 1024
CONV_ROW_CHUNK = 256

ZW_OFF, ZW_PAD = 0, 128
ZA_OFF, ZA_PAD = 128, 128
ZG_OFF, ZG_PAD = 256, 256
ZV_OFF, ZV_PAD = 512, 128
Z_WIDTH = 640


def _vmem_limit(nbytes):
    return int(min(max(nbytes + SPILL_ALLOWANCE, 16 * 1024 * 1024), V7X_VMEM_CAP))


def _nbytes(shape, dtype):
    n = 1
    for s in shape:
        n *= s
    return n * jnp.dtype(dtype).itemsize


def _pick(n, prefs):
    for p in prefs:
        if n % p == 0:
            return p
    return n


def _norm_matmul_body(x_ref, g_ref, w_ref, o_ref, xn_ref):
    @pl.when(pl.program_id(1) == 0)
    def _():
        x = x_ref[...]
        ms = jnp.mean(x * x, axis=-1, keepdims=True)
        xn_ref[...] = (x * lax.rsqrt(ms + RMS_EPS) * g_ref[...]).astype(BF16)

    o_ref[...] = jnp.dot(xn_ref[...], w_ref[...], preferred_element_type=F32).astype(o_ref.dtype)


def norm_matmul(x, gain, w, out_dtype, name):
    m, k = x.shape
    n = w.shape[1]
    tm = _pick(m, (1024, 512, 256, 128))
    tn = _pick(n, (1024, 768, 640, 512, 256, 128))
    est = (2 * _nbytes((tm, k), F32) + _nbytes((tm, k), BF16) + 2 * _nbytes((k, tn), BF16)
           + 2 * _nbytes((tm, tn), out_dtype) + 2 * _nbytes((tm, tn), F32) + _nbytes((tm, k), F32))
    return pl.pallas_call(
        _norm_matmul_body,
        grid=(m // tm, n // tn),
        in_specs=[
            pl.BlockSpec((tm, k), lambda i, j: (i, 0)),
            pl.BlockSpec((1, k), lambda i, j: (0, 0)),
            pl.BlockSpec((k, tn), lambda i, j: (0, j)),
        ],
        out_specs=pl.BlockSpec((tm, tn), lambda i, j: (i, j)),
        out_shape=jax.ShapeDtypeStruct((m, n), out_dtype),
        scratch_shapes=[pltpu.VMEM((tm, k), BF16)],
        compiler_params=pltpu.CompilerParams(
            dimension_semantics=("parallel", "arbitrary"), vmem_limit_bytes=_vmem_limit(est)),
        name=name,
    )(x, gain.reshape(1, k), w)


def _conv3_ext(u, w):
    return w[0:1, :] * pltpu.roll(u, 2, 0) + w[1:2, :] * pltpu.roll(u, 1, 0) + w[2:3, :] * u


def _norm_matmul_conv_body(x_ref, g_ref, w_ref, cw_ref, o_ref, xn_ref, carry_ref, *,
                           n_groups, tiles_per_seq, combine):
    i = pl.program_id(0)
    j = pl.program_id(1)
    tm, tn = o_ref.shape
    sub = V7X_SUBLANES
    rc = min(tm, CONV_ROW_CHUNK)

    @pl.when(j == 0)
    def _():
        x = x_ref[...]
        ms = jnp.mean(x * x, axis=-1, keepdims=True)
        xn_ref[...] = (x * lax.rsqrt(ms + RMS_EPS) * g_ref[...]).astype(BF16)

    @pl.when(i % tiles_per_seq == 0)
    def _():
        carry_ref[j] = jnp.zeros(carry_ref.shape[1:], F32)

    h = jnp.dot(xn_ref[...], w_ref[...], preferred_element_type=F32)
    cw = cw_ref[...]
    for c in range(tm // rc):
        r0 = c * rc
        if c == 0:
            ext = jnp.concatenate([carry_ref[j], h[:rc, :]], axis=0)
        else:
            ext = h[r0 - sub:r0 + rc, :]
        groups = [ext[:, g * tn:(g + 1) * tn] for g in range(n_groups)]
        taps = [cw[:, g * tn:(g + 1) * tn] for g in range(cw.shape[1] // tn)]
        o_ref[r0:r0 + rc, :] = combine(groups, taps)[sub:, :].astype(o_ref.dtype)
    carry_ref[j] = h[tm - sub:, :]


def _ffn_combine(groups, taps):
    return jax.nn.silu(_conv3_ext(groups[0], taps[0])) * _conv3_ext(groups[1], taps[1])


def _short_conv_combine(groups, taps):
    return groups[0] * _conv3_ext(groups[1] * groups[2], taps[0])


def norm_matmul_conv(x, gain, w, conv_w, seq, n_groups, combine, name):
    m, k = x.shape
    n = w.shape[1] // n_groups
    tm = _pick(seq, (1024, 512, 256, 128))
    tn = _pick(n, (512, 256, 128))
    nj = n // tn
    n_cw = conv_w.shape[1] // n
    est = (2 * _nbytes((tm, k), F32) + _nbytes((tm, k), BF16) + 2 * n_groups * _nbytes((k, tn), BF16)
           + 2 * _nbytes((tm, tn), BF16) + (n_groups + 2) * _nbytes((tm, tn), F32)
           + 8 * n_groups * _nbytes((CONV_ROW_CHUNK, tn), F32) + nj * n_groups * _nbytes((V7X_SUBLANES, tn), F32))
    return pl.pallas_call(
        functools.partial(_norm_matmul_conv_body, n_groups=n_groups, tiles_per_seq=seq // tm, combine=combine),
        grid=(m // tm, nj),
        in_specs=[
            pl.BlockSpec((tm, k), lambda i, j: (i, 0)),
            pl.BlockSpec((1, k), lambda i, j: (0, 0)),
            pl.BlockSpec((k, n_groups * tn), lambda i, j: (0, j)),
            pl.BlockSpec((CONV_WIDTH, n_cw * tn), lambda i, j: (0, j)),
        ],
        out_specs=pl.BlockSpec((tm, tn), lambda i, j: (i, j)),
        out_shape=jax.ShapeDtypeStruct((m, n), BF16),
        scratch_shapes=[pltpu.VMEM((tm, k), BF16),
                        pltpu.VMEM((nj, V7X_SUBLANES, n_groups * tn), F32)],
        compiler_params=pltpu.CompilerParams(
            dimension_semantics=("arbitrary", "arbitrary"), vmem_limit_bytes=_vmem_limit(est)),
        name=name,
    )(x, gain.reshape(1, k), _interleave_groups(w, n_groups, tn).astype(BF16),
      _interleave_groups(conv_w, n_cw, tn))


def _interleave_groups(w, n_groups, tn):
    k, gn = w.shape
    nj = gn // (n_groups * tn)
    return w.reshape(k, n_groups, nj, tn).transpose(0, 2, 1, 3).reshape(k, gn)


def _matmul_res_body(a_ref, w_ref, r_ref, o_ref):
    o_ref[...] = r_ref[...] + jnp.dot(a_ref[...], w_ref[...], preferred_element_type=F32)


def matmul_residual(a, w, res, name):
    m, k = a.shape
    n = w.shape[1]
    tm = _pick(m, (1024, 512, 256, 128))
    tn = _pick(n, (512, 256, 128)) if k > 4096 else _pick(n, (1024, 512, 256, 128))
    est = (2 * _nbytes((tm, k), BF16) + 2 * _nbytes((k, tn), BF16) + 6 * _nbytes((tm, tn), F32))
    return pl.pallas_call(
        _matmul_res_body,
        grid=(m // tm, n // tn),
        in_specs=[
            pl.BlockSpec((tm, k), lambda i, j: (i, 0)),
            pl.BlockSpec((k, tn), lambda i, j: (0, j)),
            pl.BlockSpec((tm, tn), lambda i, j: (i, j)),
        ],
        out_specs=pl.BlockSpec((tm, tn), lambda i, j: (i, j)),
        out_shape=jax.ShapeDtypeStruct((m, n), F32),
        compiler_params=pltpu.CompilerParams(
            dimension_semantics=("parallel", "parallel"), vmem_limit_bytes=_vmem_limit(est)),
        name=name,
    )(a, w, res)


def _mix_out_body(ya_ref, yb_ref, g_ref, wa_ref, wb_ref, r_ref, o_ref):
    yb = (yb_ref[...] * g_ref[...]).astype(BF16)
    acc = jnp.dot(ya_ref[...], wa_ref[...], preferred_element_type=F32)
    acc = acc + jnp.dot(yb, wb_ref[...], preferred_element_type=F32)
    o_ref[...] = r_ref[...] + acc


def mix_out_proj(ya, yb, g, w_out, res, name):
    m, ka = ya.shape
    kb = yb.shape[1]
    n = w_out.shape[1]
    tm = _pick(m, (512, 256, 128))
    tn = _pick(n, (1024, 512, 256, 128))
    est = (2 * _nbytes((tm, ka), BF16) + 4 * _nbytes((tm, kb), F32) + 2 * _nbytes((ka + kb, tn), BF16)
           + 6 * _nbytes((tm, tn), F32) + 2 * _nbytes((tm, kb), F32))
    return pl.pallas_call(
        _mix_out_body,
        grid=(m // tm, n // tn),
        in_specs=[
            pl.BlockSpec((tm, ka), lambda i, j: (i, 0)),
            pl.BlockSpec((tm, kb), lambda i, j: (i, 0)),
            pl.BlockSpec((tm, kb), lambda i, j: (i, 0)),
            pl.BlockSpec((ka, tn), lambda i, j: (0, j)),
            pl.BlockSpec((kb, tn), lambda i, j: (0, j)),
            pl.BlockSpec((tm, tn), lambda i, j: (i, j)),
        ],
        out_specs=pl.BlockSpec((tm, tn), lambda i, j: (i, j)),
        out_shape=jax.ShapeDtypeStruct((m, n), F32),
        compiler_params=pltpu.CompilerParams(
            dimension_semantics=("parallel", "parallel"), vmem_limit_bytes=_vmem_limit(est)),
        name=name,
    )(ya, yb, g, w_out[:ka], w_out[ka:], res)


def _shift_rows(u, s):
    rolled = pltpu.roll(u, s, 0)
    row = lax.broadcasted_iota(jnp.int32, u.shape, 0)
    return jnp.where(row >= s, rolled, 0.0)


def _causal_conv3(u, w):
    out = w[0:1, :] * _shift_rows(u, 2)
    out = out + w[1:2, :] * _shift_rows(u, 1)
    return out + w[2:3, :] * u


def _ffn_conv_body(hg_ref, hv_ref, wg_ref, wv_ref, o_ref):
    gate = _causal_conv3(hg_ref[0].astype(F32), wg_ref[...])
    val = _causal_conv3(hv_ref[0].astype(F32), wv_ref[...])
    o_ref[0] = (jax.nn.silu(gate) * val).astype(o_ref.dtype)


def ffn_conv_gate(h, conv_w, name):
    b, s, f2 = h.shape
    f = f2 // 2
    tc = _pick(f, (256, 128))
    nj = f // tc
    est = 4 * _nbytes((s, tc), h.dtype) + 2 * _nbytes((s, tc), BF16) + 10 * _nbytes((s, tc), F32)
    return pl.pallas_call(
        _ffn_conv_body,
        grid=(b, nj),
        in_specs=[
            pl.BlockSpec((1, s, tc), lambda i, j: (i, 0, j)),
            pl.BlockSpec((1, s, tc), lambda i, j: (i, 0, nj + j)),
            pl.BlockSpec((CONV_WIDTH, tc), lambda i, j: (0, j)),
            pl.BlockSpec((CONV_WIDTH, tc), lambda i, j: (0, nj + j)),
        ],
        out_specs=pl.BlockSpec((1, s, tc), lambda i, j: (i, 0, j)),
        out_shape=jax.ShapeDtypeStruct((b, s, f), BF16),
        compiler_params=pltpu.CompilerParams(
            dimension_semantics=("parallel", "parallel"), vmem_limit_bytes=_vmem_limit(est)),
        name=name,
    )(h, h, conv_w, conv_w)


def _short_conv_body(hb_ref, hc_ref, hu_ref, w_ref, o_ref):
    cu = hc_ref[0].astype(F32) * hu_ref[0].astype(F32)
    o_ref[0] = (hb_ref[0].astype(F32) * _causal_conv3(cu, w_ref[...])).astype(o_ref.dtype)


def short_conv_gate(h, conv_w, name):
    b, s, c3 = h.shape
    c = c3 // 3
    tc = _pick(c, (256, 128))
    nj = c // tc
    est = 6 * _nbytes((s, tc), h.dtype) + 2 * _nbytes((s, tc), BF16) + 8 * _nbytes((s, tc), F32)
    return pl.pallas_call(
        _short_conv_body,
        grid=(b, nj),
        in_specs=[
            pl.BlockSpec((1, s, tc), lambda i, j: (i, 0, j)),
            pl.BlockSpec((1, s, tc), lambda i, j: (i, 0, nj + j)),
            pl.BlockSpec((1, s, tc), lambda i, j: (i, 0, 2 * nj + j)),
            pl.BlockSpec((CONV_WIDTH, tc), lambda i, j: (0, j)),
        ],
        out_specs=pl.BlockSpec((1, s, tc), lambda i, j: (i, 0, j)),
        out_shape=jax.ShapeDtypeStruct((b, s, c), BF16),
        compiler_params=pltpu.CompilerParams(
            dimension_semantics=("parallel", "parallel"), vmem_limit_bytes=_vmem_limit(est)),
        name=name,
    )(h, h, h, conv_w)


def _head_rms(x, gain):
    return x * lax.rsqrt(jnp.mean(x * x, axis=-1, keepdims=True) + RMS_EPS) * gain


def _moba_query_block(ii, q, q_gain, slope, rel_t, kn_ref, vt_ref, km_ref):
    bs = MOBA_BLOCK
    hd = A_HEAD_DIM
    qn = _head_rms(q, q_gain)
    qs = (qn * (hd ** -0.5 * LOG2E)).astype(BF16)
    slope2 = slope * LOG2E
    bias_rel = slope2 * rel_t
    gt = lax.dot_general(km_ref[...], qn, (((1,), (1,)), ((), ())),
                         precision=lax.Precision.HIGHEST, preferred_element_type=F32)
    blk = lax.broadcasted_iota(jnp.int32, gt.shape, 0)
    gm = jnp.where(blk < ii, gt, NEG_INF)
    scores = []
    m_all = None
    for n in range(ii + 1):
        s = lax.dot_general(kn_ref[n * bs:(n + 1) * bs, :], qs, (((1,), (1,)), ((), ())),
                            preferred_element_type=F32) - bias_rel
        if n == ii:
            s = jnp.where(rel_t >= 0, s, NEG_INF)
            shift = jnp.zeros((1, bs), F32)
        else:
            g_n = gm[n:n + 1, :]
            beats = jnp.where((gm > g_n) | ((gm == g_n) & (blk < n)), 1.0, 0.0)
            chosen = jnp.sum(beats, axis=0, keepdims=True) < float(MOBA_TOPK)
            shift = jnp.where(chosen, slope2 * float((ii - n) * bs), -NEG_INF)
        m_blk = jnp.max(s, axis=0, keepdims=True) - shift
        m_all = m_blk if m_all is None else jnp.maximum(m_all, m_blk)
        scores.append((s, shift))
    acc = None
    for n, (s, shift) in enumerate(scores):
        p = jnp.exp2(s - (m_all + shift)).astype(BF16)
        pv = jnp.dot(vt_ref[:, n * bs:(n + 1) * bs], p, preferred_element_type=F32)
        acc = pv if acc is None else acc + pv
    return (acc[:hd] / acc[hd:hd + 1]).T


def _moba_body(qlo_ref, qhi_ref, k_ref, v_ref, qg_ref, kg_ref, olo_ref, ohi_ref, kn_ref, vt_ref, km_ref,
               *, n_heads, n_blocks):
    bs = MOBA_BLOCK
    hd = A_HEAD_DIM
    hp = pl.program_id(1)
    j = pl.program_id(2)

    @pl.when(j == 0)
    def _():
        km_ref[...] = jnp.zeros_like(km_ref)
        for e in range(MOBA_HEADS_PER_STEP):
            cols = slice(e * hd, (e + 1) * hd)
            for nb in range(n_blocks):
                rows = slice(nb * bs, (nb + 1) * bs)
                kn = _head_rms(k_ref[0, rows, cols], kg_ref[...])
                kn_ref[e, rows, :] = kn.astype(BF16)
                km_ref[e, nb:nb + 1, :] = jnp.mean(kn, axis=0, keepdims=True)
                vt_ref[e, :hd, rows] = v_ref[0, rows, cols].T.astype(BF16)
            vt_ref[e, hd:, :] = jnp.ones((MOBA_DENOM_ROWS, vt_ref.shape[2]), BF16)

    rel_t = (lax.broadcasted_iota(jnp.int32, (bs, bs), 1)
             - lax.broadcasted_iota(jnp.int32, (bs, bs), 0)).astype(F32)

    for jj in range(n_blocks // 2):
        @pl.when(j == jj)
        def _(jj=jj):
            for e in range(MOBA_HEADS_PER_STEP):
                cols = slice(e * hd, (e + 1) * hd)
                head = (hp * MOBA_HEADS_PER_STEP + e + 1).astype(F32)
                slope = jnp.exp2(jnp.full((1, bs), -8.0 / n_heads, F32) * head)
                for q_ref, o_ref, ii in ((qlo_ref, olo_ref, jj), (qhi_ref, ohi_ref, n_blocks - 1 - jj)):
                    o = _moba_query_block(ii, q_ref[0, :, cols], qg_ref[...], slope, rel_t,
                                          kn_ref.at[e], vt_ref.at[e], km_ref.at[e])
                    o_ref[0, :, cols] = o.astype(o_ref.dtype)


def moba_attention(proj, q_gain, k_gain, name):
    b, s, _ = proj.shape
    assert s % MOBA_BLOCK == 0
    nb = s // MOBA_BLOCK
    assert nb % 2 == 0 and A_HEADS % MOBA_HEADS_PER_STEP == 0
    nbp = -(-nb // V7X_SUBLANES) * V7X_SUBLANES
    hd = A_HEAD_DIM
    hw = MOBA_HEADS_PER_STEP * hd
    groups = A_HEADS // MOBA_HEADS_PER_STEP
    half = nb // 2
    est = (8 * _nbytes((MOBA_BLOCK, hw), F32) + 4 * _nbytes((s, hw), F32) + 2 * _nbytes((s, hw), BF16)
           + MOBA_HEADS_PER_STEP * (4 * nb + 16) * _nbytes((MOBA_BLOCK, MOBA_BLOCK), F32))
    o_lo, o_hi = pl.pallas_call(
        functools.partial(_moba_body, n_heads=A_HEADS, n_blocks=nb),
        grid=(b, groups, half),
        in_specs=[
            pl.BlockSpec((1, MOBA_BLOCK, hw), lambda bi, g, j: (bi, j, g)),
            pl.BlockSpec((1, MOBA_BLOCK, hw), lambda bi, g, j: (bi, nb - 1 - j, g)),
            pl.BlockSpec((1, s, hw), lambda bi, g, j: (bi, 0, groups + g)),
            pl.BlockSpec((1, s, hw), lambda bi, g, j: (bi, 0, 2 * groups + g)),
            pl.BlockSpec((1, hd), lambda bi, g, j: (0, 0)),
            pl.BlockSpec((1, hd), lambda bi, g, j: (0, 0)),
        ],
        out_specs=[
            pl.BlockSpec((1, MOBA_BLOCK, hw), lambda bi, g, j: (bi, j, g)),
            pl.BlockSpec((1, MOBA_BLOCK, hw), lambda bi, g, j: (bi, half - 1 - j, g)),
        ],
        out_shape=[jax.ShapeDtypeStruct((b, s // 2, A_WIDTH), BF16)] * 2,
        scratch_shapes=[
            pltpu.VMEM((MOBA_HEADS_PER_STEP, s, hd), BF16),
            pltpu.VMEM((MOBA_HEADS_PER_STEP, hd + MOBA_DENOM_ROWS, s), BF16),
            pltpu.VMEM((MOBA_HEADS_PER_STEP, nbp, hd), F32),
        ],
        compiler_params=pltpu.CompilerParams(
            dimension_semantics=("parallel", "parallel", "arbitrary"), vmem_limit_bytes=_vmem_limit(est)),
        name=name,
    )(proj, proj, proj, proj, q_gain.reshape(1, hd), k_gain.reshape(1, hd))
    return jnp.concatenate([o_lo, o_hi], axis=1)


def _token_shift(p, prev_last, mu):
    rolled = pltpu.roll(p, 1, 0)
    row = lax.broadcasted_iota(jnp.int32, p.shape, 0)
    p_prev = jnp.where(row == 0, prev_last, rolled)
    return p + mu * (p_prev - p)


def _rwkv_prep_body(*refs, has_vres):
    if has_vres:
        (pm_ref, pz_ref, mum_ref, muz_ref, w0_ref, wl_ref, a0_ref, al_ref, gl_ref, v0_ref, vl_ref, vf_ref,
         r_ref, d_ref, k_ref, v_ref, a_ref, g_ref, cm_ref, cz_ref) = refs
    else:
        (pm_ref, pz_ref, mum_ref, muz_ref, w0_ref, wl_ref, a0_ref, al_ref, gl_ref,
         r_ref, d_ref, k_ref, v_ref, a_ref, g_ref, cm_ref, cz_ref) = refs
    tm = pm_ref.shape[1]
    bw = B_WIDTH

    @pl.when(pl.program_id(1) == 0)
    def _():
        cm_ref[...] = jnp.zeros_like(cm_ref)
        cz_ref[...] = jnp.zeros_like(cz_ref)

    pm = pm_ref[0]
    pz = pz_ref[0]
    last = V7X_SUBLANES - 1
    pms = _token_shift(pm, cm_ref[last:last + 1, :], mum_ref[...])
    pzs = _token_shift(pz, cz_ref[last:last + 1, :], muz_ref[...])
    cm_ref[...] = pm[tm - V7X_SUBLANES:, :]
    cz_ref[...] = pz[tm - V7X_SUBLANES:, :]

    r_ref[0] = pms[:, :bw]
    k_ref[0] = pms[:, bw:2 * bw]
    v = pms[:, 2 * bw:3 * bw]

    z_w = jnp.tanh(pzs[:, ZW_OFF:ZW_OFF + ZW_PAD]).astype(BF16)
    w = w0_ref[...] + jnp.dot(z_w, wl_ref[...], preferred_element_type=F32)
    w = -jax.nn.softplus(-w) - 0.5
    d_ref[0] = jnp.exp(-jnp.exp(w))

    z_a = pzs[:, ZA_OFF:ZA_OFF + ZA_PAD].astype(BF16)
    a_ref[0] = jax.nn.sigmoid(a0_ref[...] + jnp.dot(z_a, al_ref[...], preferred_element_type=F32))

    z_g = jax.nn.sigmoid(pzs[:, ZG_OFF:ZG_OFF + ZG_PAD]).astype(BF16)
    g_ref[0] = jnp.dot(z_g, gl_ref[...], preferred_element_type=F32)

    if has_vres:
        z_v = pzs[:, ZV_OFF:ZV_OFF + ZV_PAD].astype(BF16)
        mix = jax.nn.sigmoid(v0_ref[...] + jnp.dot(z_v, vl_ref[...], preferred_element_type=F32))
        v = v + (vf_ref[0] - v) * mix
    v_ref[0] = v


def _pad_rows(w, rows):
    return jnp.pad(w, ((0, rows - w.shape[0]), (0, 0)))


def rwkv_prep(proj, pz, shift_mu, w0, w_lora, a0, a_lora, g_lora, v0, v_lora, v_first, name):
    b, s, _ = proj.shape
    bw = B_WIDTH
    has_vres = v_lora is not None
    tm = _pick(s, (256, 128))
    main_blk = 3 * bw
    assert (3 * A_WIDTH) % main_blk == 0

    mu_main = shift_mu[:3 * bw].reshape(1, 3 * bw)
    lora_mu = shift_mu[3 * bw:]
    mu_z = jnp.zeros((Z_WIDTH,), F32)
    mu_z = mu_z.at[ZW_OFF:ZW_OFF + LORA_W].set(lora_mu[:LORA_W])
    mu_z = mu_z.at[ZA_OFF:ZA_OFF + LORA_A].set(lora_mu[LORA_W:LORA_W + LORA_A])
    mu_z = mu_z.at[ZG_OFF:ZG_OFF + LORA_G].set(lora_mu[LORA_W + LORA_A:LORA_W + LORA_A + LORA_G])
    if has_vres:
        mu_z = mu_z.at[ZV_OFF:ZV_OFF + LORA_V].set(lora_mu[LORA_W + LORA_A + LORA_G:])
    mu_z = mu_z.reshape(1, Z_WIDTH)

    row = lambda t: t.reshape(1, bw)
    const = lambda shape: pl.BlockSpec(shape, lambda bi, i: (0, 0))
    tok = pl.BlockSpec((1, tm, bw), lambda bi, i: (bi, i, 0))
    args = [proj, pz, mu_main, mu_z, row(w0), _pad_rows(w_lora, ZW_PAD).astype(BF16),
            row(a0), _pad_rows(a_lora, ZA_PAD).astype(BF16), _pad_rows(g_lora, ZG_PAD).astype(BF16)]
    in_specs = [
        pl.BlockSpec((1, tm, main_blk), lambda bi, i: (bi, i, (3 * A_WIDTH) // main_blk)),
        pl.BlockSpec((1, tm, Z_WIDTH), lambda bi, i: (bi, i, 0)),
        const((1, main_blk)), const((1, Z_WIDTH)),
        const((1, bw)), const((ZW_PAD, bw)), const((1, bw)), const((ZA_PAD, bw)), const((ZG_PAD, bw)),
    ]
    if has_vres:
        args += [row(v0), _pad_rows(v_lora, ZV_PAD).astype(BF16), v_first]
        in_specs += [const((1, bw)), const((ZV_PAD, bw)), tok]
    out_sds = jax.ShapeDtypeStruct((b, s, bw), F32)
    est = (2 * _nbytes((tm, main_blk + Z_WIDTH), F32) + 16 * _nbytes((tm, bw), F32)
           + 6 * _nbytes((tm, main_blk), F32) + 4 * _nbytes((ZG_PAD + 3 * ZW_PAD, bw), BF16))
    return pl.pallas_call(
        functools.partial(_rwkv_prep_body, has_vres=has_vres),
        grid=(b, s // tm),
        in_specs=in_specs,
        out_specs=[tok] * 6,
        out_shape=[out_sds] * 6,
        scratch_shapes=[pltpu.VMEM((V7X_SUBLANES, main_blk), F32), pltpu.VMEM((V7X_SUBLANES, Z_WIDTH), F32)],
        compiler_params=pltpu.CompilerParams(
            dimension_semantics=("parallel", "arbitrary"), vmem_limit_bytes=_vmem_limit(est)),
        name=name,
    )(*args)


def _rwkv_scan_body(r_ref, d_ref, k_ref, v_ref, a_ref, kk_ref, ka_ref, rk_ref, gw_ref, gb_ref,
                    o_ref, st_ref, y_ref):
    tc = r_ref.shape[0]
    nd = B_HEAD_DIM

    @pl.when(pl.program_id(0) == 0)
    def _():
        st_ref[...] = jnp.zeros_like(st_ref)

    def step(t, carry):
        r_t = r_ref[t]
        d_t = d_ref[t]
        k_t = k_ref[t]
        v_t = v_ref[t]
        a_t = a_ref[t]
        kk = k_t * kk_ref[...]
        nrm = jnp.sqrt(jnp.sum(kk * kk, axis=0, keepdims=True))
        kk = kk / jnp.maximum(nrm, 1e-12)
        k_m = k_t * (1.0 + (a_t - 1.0) * ka_ref[...])
        a_s = -kk
        b_s = kk * a_t
        for vi in range(nd):
            s_v = st_ref[vi]
            sa = jnp.sum(s_v * a_s, axis=0, keepdims=True)
            s_v = s_v * d_t + sa * b_s + v_t[vi:vi + 1, :] * k_m
            st_ref[vi] = s_v
            y_ref[vi:vi + 1, :] = jnp.sum(s_v * r_t, axis=0, keepdims=True)
        y = y_ref[...]
        mean = jnp.mean(y, axis=0, keepdims=True)
        var = jnp.mean(jnp.square(y - mean), axis=0, keepdims=True)
        y = (y - mean) * lax.rsqrt(var + GN_EPS) * gw_ref[...] + gb_ref[...]
        bonus = jnp.sum(r_t * k_m * rk_ref[...], axis=0, keepdims=True) * v_t
        o_ref[t] = y + bonus
        return carry

    lax.fori_loop(0, tc, step, 0)


def rwkv_scan(r, d, k, v, a, k_k, k_a, r_k, gn_w, gn_b, name):
    s, nd, n = r.shape
    tc = _pick(s, (64, 32, 16, 8))
    seq = pl.BlockSpec((tc, nd, n), lambda i: (i, 0, 0))
    par = pl.BlockSpec((nd, n), lambda i: (0, 0))
    est = 12 * _nbytes((tc, nd, n), F32) + _nbytes((nd, nd, n), F32) + 16 * _nbytes((nd, n), F32)
    return pl.pallas_call(
        _rwkv_scan_body,
        grid=(s // tc,),
        in_specs=[seq] * 5 + [par] * 5,
        out_specs=seq,
        out_shape=jax.ShapeDtypeStruct((s, nd, n), F32),
        scratch_shapes=[pltpu.VMEM((nd, nd, n), F32), pltpu.VMEM((nd, n), F32)],
        compiler_params=pltpu.CompilerParams(
            dimension_semantics=("arbitrary",), vmem_limit_bytes=_vmem_limit(est)),
        name=name,
    )(r, d, k, v, a, k_k, k_a, r_k, gn_w, gn_b)


def _to_scan_layout(t, b, s):
    return jnp.transpose(t.reshape(b, s, B_HEADS, B_HEAD_DIM), (1, 3, 0, 2)).reshape(s, B_HEAD_DIM, b * B_HEADS)


def _from_scan_layout(t, b, s):
    return jnp.transpose(t.reshape(s, B_HEAD_DIM, b, B_HEADS), (2, 0, 3, 1)).reshape(b * s, B_WIDTH)


def _param_tile(p, b):
    return jnp.tile(p.reshape(B_HEADS, B_HEAD_DIM).T, (1, b))


def _lora_proj_weight(w_in, has_vres):
    o = 3 * A_WIDTH + 3 * B_WIDTH
    d = w_in.shape[0]
    wz = jnp.zeros((d, Z_WIDTH), w_in.dtype)
    wz = wz.at[:, ZW_OFF:ZW_OFF + LORA_W].set(w_in[:, o:o + LORA_W])
    o += LORA_W
    wz = wz.at[:, ZA_OFF:ZA_OFF + LORA_A].set(w_in[:, o:o + LORA_A])
    o += LORA_A
    wz = wz.at[:, ZG_OFF:ZG_OFF + LORA_G].set(w_in[:, o:o + LORA_G])
    o += LORA_G
    if has_vres:
        wz = wz.at[:, ZV_OFF:ZV_OFF + LORA_V].set(w_in[:, o:o + LORA_V])
    return wz


def moba_rwkv_layer(x, b, s, li, norm_mix, w_in, q_gain, k_gain, shift_mu, w0, w_lora, a0, a_lora, g_lora,
                    k_k, k_a, r_k, gn_w, gn_b, w_out, v0, v_lora, v_first):
    has_vres = v_lora is not None
    tag = "l%d_" % li
    n_main = 3 * A_WIDTH + 3 * B_WIDTH
    proj = norm_matmul(x, norm_mix, w_in[:, :n_main].astype(BF16), F32, tag + "in_proj")
    pz = norm_matmul(x, norm_mix, _lora_proj_weight(w_in, has_vres).astype(BF16), F32, tag + "lora_proj")
    proj = proj.reshape(b, s, n_main)
    pz = pz.reshape(b, s, Z_WIDTH)

    y_a = moba_attention(proj, q_gain, k_gain, tag + "moba")

    r, d, k, v, a, g = rwkv_prep(proj, pz, shift_mu, w0, w_lora, a0, a_lora, g_lora, v0, v_lora, v_first,
                                 tag + "rwkv_prep")
    if not has_vres:
        v_first = v
    tiles = [_param_tile(p, b) for p in (k_k, k_a, r_k.reshape(-1), gn_w, gn_b)]
    y_b = rwkv_scan(*[_to_scan_layout(t, b, s) for t in (r, d, k, v, a)], *tiles, tag + "rwkv_scan")
    y_b = _from_scan_layout(y_b, b, s)

    x = mix_out_proj(y_a.reshape(b * s, A_WIDTH), y_b, g.reshape(b * s, B_WIDTH), w_out.astype(BF16), x,
                     tag + "out_proj")
    return x, v_first


def short_conv_layer(x, b, s, li, norm_mix, conv_in, conv_w, conv_out):
    tag = "l%d_" % li
    m = norm_matmul_conv(x, norm_mix, conv_in, conv_w, s, 3, _short_conv_combine, tag + "conv_in")
    return matmul_residual(m, conv_out.astype(BF16), x, tag + "conv_out")


def conv_ffn(x, b, s, li, norm_ffn, ffn_up, ffn_conv, ffn_down):
    tag = "l%d_" % li
    m = norm_matmul_conv(x, norm_ffn, ffn_up, ffn_conv, s, 2, _ffn_combine, tag + "ffn_up")
    return matmul_residual(m, ffn_down.astype(BF16), x, tag + "ffn_down")


def kernel(x, l0_norm_mix, l0_w_in, l0_q_gain, l0_k_gain, l0_shift_mu, l0_w0, l0_w_lora, l0_a0, l0_a_lora, l0_g_lora, l0_k_k, l0_k_a, l0_r_k, l0_gn_w, l0_gn_b, l0_w_out, l0_norm_ffn, l0_ffn_up, l0_ffn_conv, l0_ffn_down, l1_norm_mix, l1_conv_in, l1_conv_w, l1_conv_out, l1_norm_ffn, l1_ffn_up, l1_ffn_conv, l1_ffn_down, l2_norm_mix, l2_w_in, l2_q_gain, l2_k_gain, l2_shift_mu, l2_w0, l2_w_lora, l2_a0, l2_a_lora, l2_g_lora, l2_k_k, l2_k_a, l2_r_k, l2_gn_w, l2_gn_b, l2_w_out, l2_v0, l2_v_lora, l2_norm_ffn, l2_ffn_up, l2_ffn_conv, l2_ffn_down, l3_norm_mix, l3_conv_in, l3_conv_w, l3_conv_out, l3_norm_ffn, l3_ffn_up, l3_ffn_conv, l3_ffn_down):
    b, s, d = x.shape
    h = x.reshape(b * s, d)

    h, v_first = moba_rwkv_layer(
        h, b, s, 0, l0_norm_mix, l0_w_in, l0_q_gain, l0_k_gain, l0_shift_mu, l0_w0, l0_w_lora, l0_a0,
        l0_a_lora, l0_g_lora, l0_k_k, l0_k_a, l0_r_k, l0_gn_w, l0_gn_b, l0_w_out, None, None, None)
    h = conv_ffn(h, b, s, 0, l0_norm_ffn, l0_ffn_up, l0_ffn_conv, l0_ffn_down)

    h = short_conv_layer(h, b, s, 1, l1_norm_mix, l1_conv_in, l1_conv_w, l1_conv_out)
    h = conv_ffn(h, b, s, 1, l1_norm_ffn, l1_ffn_up, l1_ffn_conv, l1_ffn_down)

    h, _ = moba_rwkv_layer(
        h, b, s, 2, l2_norm_mix, l2_w_in, l2_q_gain, l2_k_gain, l2_shift_mu, l2_w0, l2_w_lora, l2_a0,
        l2_a_lora, l2_g_lora, l2_k_k, l2_k_a, l2_r_k, l2_gn_w, l2_gn_b, l2_w_out, l2_v0, l2_v_lora, v_first)
    h = conv_ffn(h, b, s, 2, l2_norm_ffn, l2_ffn_up, l2_ffn_conv, l2_ffn_down)

    h = short_conv_layer(h, b, s, 3, l3_norm_mix, l3_conv_in, l3_conv_w, l3_conv_out)
    h = conv_ffn(h, b, s, 3, l3_norm_ffn, l3_ffn_up, l3_ffn_conv, l3_ffn_down)
    return h.reshape(b, s, d)
```

```python
import functools

import jax
import jax.numpy as jnp
from jax import lax
from jax.experimental import pallas as pl
from jax.experimental.pallas import tpu as pltpu

F32 = jnp.float32
BF16 = jnp.bfloat16

A_HEADS = 8
A_HEAD_DIM = 128
A_WIDTH = A_HEADS * A_HEAD_DIM
MOBA_BLOCK = 256
MOBA_TOPK = 3
MOBA_HEADS_PER_STEP = 2
MOBA_DENOM_ROWS = 16
B_HEADS = 16
B_HEAD_DIM = 64
B_WIDTH = B_HEADS * B_HEAD_DIM
RWKV_CHUNK = 64
LORA_W = 64
LORA_A = 64
LORA_G = 160
LORA_V = 32
CONV_WIDTH = 3
RMS_EPS = 1e-6
GN_EPS = 64e-5
NEG_INF = -1e30
LOG2E = 1.4426950408889634

V7X_LANES = 128
V7X_SUBLANES = 8
V7X_VMEM_BYTES = 64 * 1024 * 1024
V7X_VMEM_CAP = V7X_VMEM_BYTES - 8 * 1024 * 1024
SPILL_ALLOWANCE = 4 * 1024 * 1024
CONV_ROW_CHUNK = 256

ZW_OFF, ZW_PAD = 0, 128
ZA_OFF, ZA_PAD = 128, 128
ZG_OFF, ZG_PAD = 256, 256
ZV_OFF, ZV_PAD = 512, 128
Z_WIDTH = 640


def _vmem_limit(nbytes):
    return int(min(max(nbytes + SPILL_ALLOWANCE, 16 * 1024 * 1024), V7X_VMEM_CAP))


def _nbytes(shape, dtype):
    n = 1
    for s in shape:
        n *= s
    return n * jnp.dtype(dtype).itemsize


def _pick(n, prefs):
    for p in prefs:
        if n % p == 0:
            return p
    return n


def _norm_matmul_body(x_ref, g_ref, w_ref, o_ref, xn_ref):
    @pl.when(pl.program_id(1) == 0)
    def _():
        x = x_ref[...]
        ms = jnp.mean(x * x, axis=-1, keepdims=True)
        xn_ref[...] = (x * lax.rsqrt(ms + RMS_EPS) * g_ref[...]).astype(BF16)

    o_ref[...] = jnp.dot(xn_ref[...], w_ref[...], preferred_element_type=F32).astype(o_ref.dtype)


def norm_matmul(x, gain, w, out_dtype, name):
    m, k = x.shape
    n = w.shape[1]
    tm = _pick(m, (1024, 512, 256, 128))
    tn = _pick(n, (1024, 768, 640, 512, 256, 128))
    est = (2 * _nbytes((tm, k), F32) + _nbytes((tm, k), BF16) + 2 * _nbytes((k, tn), BF16)
           + 2 * _nbytes((tm, tn), out_dtype) + 2 * _nbytes((tm, tn), F32) + _nbytes((tm, k), F32))
    return pl.pallas_call(
        _norm_matmul_body,
        grid=(m // tm, n // tn),
        in_specs=[
            pl.BlockSpec((tm, k), lambda i, j: (i, 0)),
            pl.BlockSpec((1, k), lambda i, j: (0, 0)),
            pl.BlockSpec((k, tn), lambda i, j: (0, j)),
        ],
        out_specs=pl.BlockSpec((tm, tn), lambda i, j: (i, j)),
        out_shape=jax.ShapeDtypeStruct((m, n), out_dtype),
        scratch_shapes=[pltpu.VMEM((tm, k), BF16)],
        compiler_params=pltpu.CompilerParams(
            dimension_semantics=("parallel", "arbitrary"), vmem_limit_bytes=_vmem_limit(est)),
        name=name,
    )(x, gain.reshape(1, k), w)


def _conv3_ext(u, w):
    return w[0:1, :] * pltpu.roll(u, 2, 0) + w[1:2, :] * pltpu.roll(u, 1, 0) + w[2:3, :] * u


def _norm_matmul_conv_body(x_ref, g_ref, *refs, n_groups, n_taps, tiles_per_seq, combine):
    w_refs = refs[:n_groups]
    cw_refs = refs[n_groups:n_groups + n_taps]
    o_ref, xn_ref, carry_ref = refs[n_groups + n_taps:]
    i = pl.program_id(0)
    j = pl.program_id(1)
    tm, tn = o_ref.shape
    sub = V7X_SUBLANES
    rc = min(tm, CONV_ROW_CHUNK)

    @pl.when(j == 0)
    def _():
        x = x_ref[...]
        ms = jnp.mean(x * x, axis=-1, keepdims=True)
        xn_ref[...] = (x * lax.rsqrt(ms + RMS_EPS) * g_ref[...]).astype(BF16)

    @pl.when(i % tiles_per_seq == 0)
    def _():
        carry_ref[j] = jnp.zeros(carry_ref.shape[1:], F32)

    w = jnp.concatenate([w_ref[...] for w_ref in w_refs], axis=1)
    h = jnp.dot(xn_ref[...], w, preferred_element_type=F32)
    taps = [cw_ref[...] for cw_ref in cw_refs]
    for c in range(tm // rc):
        r0 = c * rc
        if c == 0:
            ext = jnp.concatenate([carry_ref[j], h[:rc, :]], axis=0)
        else:
            ext = h[r0 - sub:r0 + rc, :]
        groups = [ext[:, g * tn:(g + 1) * tn] for g in range(n_groups)]
        o_ref[r0:r0 + rc, :] = combine(groups, taps)[sub:, :].astype(o_ref.dtype)
    carry_ref[j] = h[tm - sub:, :]


def _ffn_combine(groups, taps):
    return jax.nn.silu(_conv3_ext(groups[0], taps[0])) * _conv3_ext(groups[1], taps[1])


def _short_conv_combine(groups, taps):
    return groups[0] * _conv3_ext(groups[1] * groups[2], taps[0])


def norm_matmul_conv(x, gain, w, conv_w, seq, n_groups, combine, name):
    m, k = x.shape
    n = w.shape[1] // n_groups
    tm = _pick(seq, (1024, 512, 256, 128))
    tn = _pick(n, (512, 256, 128))
    nj = n // tn
    n_cw = conv_w.shape[1] // n
    est = (2 * _nbytes((tm, k), F32) + _nbytes((tm, k), BF16) + 3 * n_groups * _nbytes((k, tn), BF16)
           + 2 * _nbytes((tm, tn), BF16) + (n_groups + 2) * _nbytes((tm, tn), F32)
           + 8 * n_groups * _nbytes((CONV_ROW_CHUNK, tn), F32) + nj * n_groups * _nbytes((V7X_SUBLANES, tn), F32))
    w_specs = [pl.BlockSpec((k, tn), lambda i, j, g=g: (0, g * nj + j)) for g in range(n_groups)]
    cw_specs = [pl.BlockSpec((CONV_WIDTH, tn), lambda i, j, g=g: (0, g * nj + j)) for g in range(n_cw)]
    return pl.pallas_call(
        functools.partial(_norm_matmul_conv_body, n_groups=n_groups, n_taps=n_cw, tiles_per_seq=seq // tm,
                          combine=combine),
        grid=(m // tm, nj),
        in_specs=[pl.BlockSpec((tm, k), lambda i, j: (i, 0)), pl.BlockSpec((1, k), lambda i, j: (0, 0))]
        + w_specs + cw_specs,
        out_specs=pl.BlockSpec((tm, tn), lambda i, j: (i, j)),
        out_shape=jax.ShapeDtypeStruct((m, n), BF16),
        scratch_shapes=[pltpu.VMEM((tm, k), BF16),
                        pltpu.VMEM((nj, V7X_SUBLANES, n_groups * tn), F32)],
        compiler_params=pltpu.CompilerParams(
            dimension_semantics=("arbitrary", "arbitrary"), vmem_limit_bytes=_vmem_limit(est)),
        name=name,
    )(x, gain.reshape(1, k), *([w] * n_groups), *([conv_w] * n_cw))


def _matmul_res_body(a_ref, w_ref, r_ref, o_ref):
    o_ref[...] = r_ref[...] + jnp.dot(a_ref[...], w_ref[...], preferred_element_type=F32)


def matmul_residual(a, w, res, name):
    m, k = a.shape
    n = w.shape[1]
    tm = _pick(m, (1024, 512, 256, 128))
    tn = _pick(n, (512, 256, 128)) if k > 4096 else _pick(n, (1024, 512, 256, 128))
    est = (2 * _nbytes((tm, k), BF16) + 2 * _nbytes((k, tn), BF16) + 6 * _nbytes((tm, tn), F32))
    return pl.pallas_call(
        _matmul_res_body,
        grid=(m // tm, n // tn),
        in_specs=[
            pl.BlockSpec((tm, k), lambda i, j: (i, 0)),
            pl.BlockSpec((k, tn), lambda i, j: (0, j)),
            pl.BlockSpec((tm, tn), lambda i, j: (i, j)),
        ],
        out_specs=pl.BlockSpec((tm, tn), lambda i, j: (i, j)),
        out_shape=jax.ShapeDtypeStruct((m, n), F32),
        compiler_params=pltpu.CompilerParams(
            dimension_semantics=("parallel", "parallel"), vmem_limit_bytes=_vmem_limit(est)),
        name=name,
    )(a, w, res)


def _mix_out_body(ya_ref, yb_ref, wa_ref, wb_ref, r_ref, o_ref):
    acc = jnp.dot(ya_ref[...], wa_ref[...], preferred_element_type=F32)
    acc = acc + jnp.dot(yb_ref[...], wb_ref[...], preferred_element_type=F32)
    o_ref[...] = r_ref[...] + acc


def mix_out_proj(ya, yb, w_out, res, name):
    m, ka = ya.shape
    kb = yb.shape[1]
    n = w_out.shape[1]
    tm = _pick(m, (1024, 512, 256, 128))
    tn = _pick(n, (1024, 512, 256, 128))
    est = 2 * _nbytes((tm, ka + kb), BF16) + 2 * _nbytes((ka + kb, tn), BF16) + 6 * _nbytes((tm, tn), F32)
    return pl.pallas_call(
        _mix_out_body,
        grid=(m // tm, n // tn),
        in_specs=[
            pl.BlockSpec((tm, ka), lambda i, j: (i, 0)),
            pl.BlockSpec((tm, kb), lambda i, j: (i, 0)),
            pl.BlockSpec((ka, tn), lambda i, j: (0, j)),
            pl.BlockSpec((kb, tn), lambda i, j: (0, j)),
            pl.BlockSpec((tm, tn), lambda i, j: (i, j)),
        ],
        out_specs=pl.BlockSpec((tm, tn), lambda i, j: (i, j)),
        out_shape=jax.ShapeDtypeStruct((m, n), F32),
        compiler_params=pltpu.CompilerParams(
            dimension_semantics=("parallel", "parallel"), vmem_limit_bytes=_vmem_limit(est)),
        name=name,
    )(ya, yb, w_out[:ka], w_out[ka:], res)


def _head_rms(x, gain):
    return x * lax.rsqrt(jnp.mean(x * x, axis=-1, keepdims=True) + RMS_EPS) * gain


def _moba_query_block(ii, q, q_gain, slope, rel_t, kn_ref, vt_ref, km_ref):
    bs = MOBA_BLOCK
    hd = A_HEAD_DIM
    qn = _head_rms(q, q_gain)
    qs = (qn * (hd ** -0.5 * LOG2E)).astype(BF16)
    slope2 = slope * LOG2E
    bias_rel = slope2 * rel_t
    gt = lax.dot_general(km_ref[...], qn, (((1,), (1,)), ((), ())),
                         precision=lax.Precision.HIGHEST, preferred_element_type=F32)
    blk = lax.broadcasted_iota(jnp.int32, gt.shape, 0)
    gm = jnp.where(blk < ii, gt, NEG_INF)
    scores = []
    m_all = None
    for n in range(ii + 1):
        s = lax.dot_general(kn_ref[n * bs:(n + 1) * bs, :], qs, (((1,), (1,)), ((), ())),
                            preferred_element_type=F32) - bias_rel
        if n == ii:
            s = jnp.where(rel_t >= 0, s, NEG_INF)
            shift = jnp.zeros((1, bs), F32)
        else:
            g_n = gm[n:n + 1, :]
            beats = jnp.where((gm > g_n) | ((gm == g_n) & (blk < n)), 1.0, 0.0)
            chosen = jnp.sum(beats, axis=0, keepdims=True) < float(MOBA_TOPK)
            shift = jnp.where(chosen, slope2 * float((ii - n) * bs), -NEG_INF)
        m_blk = jnp.max(s, axis=0, keepdims=True) - shift
        m_all = m_blk if m_all is None else jnp.maximum(m_all, m_blk)
        scores.append((s, shift))
    acc = None
    for n, (s, shift) in enumerate(scores):
        p = jnp.exp2(s - (m_all + shift)).astype(BF16)
        pv = jnp.dot(vt_ref[:, n * bs:(n + 1) * bs], p, preferred_element_type=F32)
        acc = pv if acc is None else acc + pv
    return (acc[:hd] / acc[hd:hd + 1]).T


def _moba_body(qlo_ref, qhi_ref, k_ref, v_ref, qg_ref, kg_ref, olo_ref, ohi_ref, kn_ref, vt_ref, km_ref,
               *, n_heads, n_blocks):
    bs = MOBA_BLOCK
    hd = A_HEAD_DIM
    hp = pl.program_id(1)
    j = pl.program_id(2)

    @pl.when(j == 0)
    def _():
        km_ref[...] = jnp.zeros_like(km_ref)
        for e in range(MOBA_HEADS_PER_STEP):
            cols = slice(e * hd, (e + 1) * hd)
            for nb in range(n_blocks):
                rows = slice(nb * bs, (nb + 1) * bs)
                kn = _head_rms(k_ref[0, rows, cols], kg_ref[...])
                kn_ref[e, rows, :] = kn.astype(BF16)
                km_ref[e, nb:nb + 1, :] = jnp.mean(kn, axis=0, keepdims=True)
                vt_ref[e, :hd, rows] = v_ref[0, rows, cols].T.astype(BF16)
            vt_ref[e, hd:, :] = jnp.ones((MOBA_DENOM_ROWS, vt_ref.shape[2]), BF16)

    rel_t = (lax.broadcasted_iota(jnp.int32, (bs, bs), 1)
             - lax.broadcasted_iota(jnp.int32, (bs, bs), 0)).astype(F32)

    for jj in range(n_blocks // 2):
        @pl.when(j == jj)
        def _(jj=jj):
            for e in range(MOBA_HEADS_PER_STEP):
                cols = slice(e * hd, (e + 1) * hd)
                head = (hp * MOBA_HEADS_PER_STEP + e + 1).astype(F32)
                slope = jnp.exp2(jnp.full((1, bs), -8.0 / n_heads, F32) * head)
                for q_ref, o_ref, ii in ((qlo_ref, olo_ref, jj), (qhi_ref, ohi_ref, n_blocks - 1 - jj)):
                    o = _moba_query_block(ii, q_ref[0, :, cols], qg_ref[...], slope, rel_t,
                                          kn_ref.at[e], vt_ref.at[e], km_ref.at[e])
                    o_ref[0, :, cols] = o.astype(o_ref.dtype)


def moba_attention(proj, q_gain, k_gain, name):
    b, s, _ = proj.shape
    assert s % MOBA_BLOCK == 0
    nb = s // MOBA_BLOCK
    assert nb % 2 == 0 and A_HEADS % MOBA_HEADS_PER_STEP == 0
    nbp = -(-nb // V7X_SUBLANES) * V7X_SUBLANES
    hd = A_HEAD_DIM
    hw = MOBA_HEADS_PER_STEP * hd
    groups = A_HEADS // MOBA_HEADS_PER_STEP
    half = nb // 2
    est = (8 * _nbytes((MOBA_BLOCK, hw), F32) + 4 * _nbytes((s, hw), F32) + 2 * _nbytes((s, hw), BF16)
           + MOBA_HEADS_PER_STEP * (4 * nb + 16) * _nbytes((MOBA_BLOCK, MOBA_BLOCK), F32))
    o_lo, o_hi = pl.pallas_call(
        functools.partial(_moba_body, n_heads=A_HEADS, n_blocks=nb),
        grid=(b, groups, half),
        in_specs=[
            pl.BlockSpec((1, MOBA_BLOCK, hw), lambda bi, g, j: (bi, j, g)),
            pl.BlockSpec((1, MOBA_BLOCK, hw), lambda bi, g, j: (bi, nb - 1 - j, g)),
            pl.BlockSpec((1, s, hw), lambda bi, g, j: (bi, 0, groups + g)),
            pl.BlockSpec((1, s, hw), lambda bi, g, j: (bi, 0, 2 * groups + g)),
            pl.BlockSpec((1, hd), lambda bi, g, j: (0, 0)),
            pl.BlockSpec((1, hd), lambda bi, g, j: (0, 0)),
        ],
        out_specs=[
            pl.BlockSpec((1, MOBA_BLOCK, hw), lambda bi, g, j: (bi, j, g)),
            pl.BlockSpec((1, MOBA_BLOCK, hw), lambda bi, g, j: (bi, half - 1 - j, g)),
        ],
        out_shape=[jax.ShapeDtypeStruct((b, s // 2, A_WIDTH), BF16)] * 2,
        scratch_shapes=[
            pltpu.VMEM((MOBA_HEADS_PER_STEP, s, hd), BF16),
            pltpu.VMEM((MOBA_HEADS_PER_STEP, hd + MOBA_DENOM_ROWS, s), BF16),
            pltpu.VMEM((MOBA_HEADS_PER_STEP, nbp, hd), F32),
        ],
        compiler_params=pltpu.CompilerParams(
            dimension_semantics=("parallel", "parallel", "arbitrary"), vmem_limit_bytes=_vmem_limit(est)),
        name=name,
    )(proj, proj, proj, proj, q_gain.reshape(1, hd), k_gain.reshape(1, hd))
    return jnp.concatenate([o_lo, o_hi], axis=1)


def _token_shift(p, prev_last, mu):
    rolled = pltpu.roll(p, 1, 0)
    row = lax.broadcasted_iota(jnp.int32, p.shape, 0)
    p_prev = jnp.where(row == 0, prev_last, rolled)
    return p + mu * (p_prev - p)


def _head_sums(x, ones_ref):
    tiles = [jnp.dot(x[:, c:c + V7X_LANES].astype(BF16), ones_ref[...], preferred_element_type=F32)
             for c in range(0, x.shape[1], V7X_LANES)]
    return jnp.concatenate(tiles, axis=1)


def _rwkv_prep_body(*refs, has_vres):
    if has_vres:
        (pm_ref, pz_ref, mum_ref, muz_ref, w0_ref, wl_ref, a0_ref, al_ref, gl_ref, kk_ref, ka_ref, rk_ref,
         ones_ref, v0_ref, vl_ref, vf_ref,
         r_ref, ld_ref, k_ref, v_ref, kn_ref, b_ref, bonus_ref, g_ref, cm_ref, cz_ref) = refs
    else:
        (pm_ref, pz_ref, mum_ref, muz_ref, w0_ref, wl_ref, a0_ref, al_ref, gl_ref, kk_ref, ka_ref, rk_ref,
         ones_ref,
         r_ref, ld_ref, k_ref, v_ref, kn_ref, b_ref, bonus_ref, g_ref, cm_ref, cz_ref) = refs
    tm = pm_ref.shape[1]
    bw = B_WIDTH

    @pl.when(pl.program_id(1) == 0)
    def _():
        cm_ref[...] = jnp.zeros_like(cm_ref)
        cz_ref[...] = jnp.zeros_like(cz_ref)

    pm = pm_ref[0]
    pz = pz_ref[0]
    last = V7X_SUBLANES - 1
    pms = _token_shift(pm, cm_ref[last:last + 1, :], mum_ref[...])
    pzs = _token_shift(pz, cz_ref[last:last + 1, :], muz_ref[...])
    cm_ref[...] = pm[tm - V7X_SUBLANES:, :]
    cz_ref[...] = pz[tm - V7X_SUBLANES:, :]

    r = pms[:, :bw]
    k = pms[:, bw:2 * bw]
    v = pms[:, 2 * bw:3 * bw]

    z_w = jnp.tanh(pzs[:, ZW_OFF:ZW_OFF + ZW_PAD]).astype(BF16)
    w = w0_ref[...] + jnp.dot(z_w, wl_ref[...], preferred_element_type=F32)
    w = -jax.nn.softplus(-w) - 0.5
    ld_ref[0] = -jnp.exp(w)

    z_a = pzs[:, ZA_OFF:ZA_OFF + ZA_PAD].astype(BF16)
    a = jax.nn.sigmoid(a0_ref[...] + jnp.dot(z_a, al_ref[...], preferred_element_type=F32))

    z_g = jax.nn.sigmoid(pzs[:, ZG_OFF:ZG_OFF + ZG_PAD]).astype(BF16)
    g_ref[0] = jnp.dot(z_g, gl_ref[...], preferred_element_type=F32)

    if has_vres:
        z_v = pzs[:, ZV_OFF:ZV_OFF + ZV_PAD].astype(BF16)
        mix = jax.nn.sigmoid(v0_ref[...] + jnp.dot(z_v, vl_ref[...], preferred_element_type=F32))
        v = v + (vf_ref[0] - v) * mix

    kk = k * kk_ref[...]
    kk = kk / jnp.maximum(jnp.sqrt(_head_sums(kk * kk, ones_ref)), 1e-12)
    k = k * (1.0 + (a - 1.0) * ka_ref[...])
    r_ref[0] = r
    k_ref[0] = k
    v_ref[0] = v
    kn_ref[0] = kk
    b_ref[0] = kk * a
    bonus_ref[0] = _head_sums(r * k * rk_ref[...], ones_ref) * v


def _pad_rows(w, rows):
    return jnp.pad(w, ((0, rows - w.shape[0]), (0, 0)))


def _head_pair_ones():
    head = jnp.arange(V7X_LANES) // B_HEAD_DIM
    return (head[:, None] == head[None, :]).astype(BF16)


def rwkv_prep(proj, pz, shift_mu, w0, w_lora, a0, a_lora, g_lora, k_k, k_a, r_k, v0, v_lora, v_first, name):
    b, s, _ = proj.shape
    bw = B_WIDTH
    has_vres = v_lora is not None
    tm = _pick(s, (256, 128))
    main_blk = 3 * bw
    assert (3 * A_WIDTH) % main_blk == 0

    mu_main = shift_mu[:3 * bw].reshape(1, 3 * bw)
    lora_mu = shift_mu[3 * bw:]
    mu_z = jnp.zeros((Z_WIDTH,), F32)
    mu_z = mu_z.at[ZW_OFF:ZW_OFF + LORA_W].set(lora_mu[:LORA_W])
    mu_z = mu_z.at[ZA_OFF:ZA_OFF + LORA_A].set(lora_mu[LORA_W:LORA_W + LORA_A])
    mu_z = mu_z.at[ZG_OFF:ZG_OFF + LORA_G].set(lora_mu[LORA_W + LORA_A:LORA_W + LORA_A + LORA_G])
    if has_vres:
        mu_z = mu_z.at[ZV_OFF:ZV_OFF + LORA_V].set(lora_mu[LORA_W + LORA_A + LORA_G:])
    mu_z = mu_z.reshape(1, Z_WIDTH)

    row = lambda t: t.reshape(1, bw)
    const = lambda shape: pl.BlockSpec(shape, lambda bi, i: (0, 0))
    tok = pl.BlockSpec((1, tm, bw), lambda bi, i: (bi, i, 0))
    args = [proj, pz, mu_main, mu_z, row(w0), _pad_rows(w_lora, ZW_PAD).astype(BF16),
            row(a0), _pad_rows(a_lora, ZA_PAD).astype(BF16), _pad_rows(g_lora, ZG_PAD).astype(BF16),
            row(k_k), row(k_a), row(r_k), _head_pair_ones()]
    in_specs = [
        pl.BlockSpec((1, tm, main_blk), lambda bi, i: (bi, i, (3 * A_WIDTH) // main_blk)),
        pl.BlockSpec((1, tm, Z_WIDTH), lambda bi, i: (bi, i, 0)),
        const((1, main_blk)), const((1, Z_WIDTH)),
        const((1, bw)), const((ZW_PAD, bw)), const((1, bw)), const((ZA_PAD, bw)), const((ZG_PAD, bw)),
        const((1, bw)), const((1, bw)), const((1, bw)), const((V7X_LANES, V7X_LANES)),
    ]
    if has_vres:
        args += [row(v0), _pad_rows(v_lora, ZV_PAD).astype(BF16), v_first]
        in_specs += [const((1, bw)), const((ZV_PAD, bw)), tok]
    out_sds = jax.ShapeDtypeStruct((b, s, bw), F32)
    n_out = 8
    est = (2 * _nbytes((tm, main_blk + Z_WIDTH), F32) + (2 * n_out + 6) * _nbytes((tm, bw), F32)
           + 6 * _nbytes((tm, main_blk), F32) + 4 * _nbytes((ZG_PAD + 3 * ZW_PAD, bw), BF16))
    return pl.pallas_call(
        functools.partial(_rwkv_prep_body, has_vres=has_vres),
        grid=(b, s // tm),
        in_specs=in_specs,
        out_specs=[tok] * n_out,
        out_shape=[out_sds] * n_out,
        scratch_shapes=[pltpu.VMEM((V7X_SUBLANES, main_blk), F32), pltpu.VMEM((V7X_SUBLANES, Z_WIDTH), F32)],
        compiler_params=pltpu.CompilerParams(
            dimension_semantics=("parallel", "arbitrary"), vmem_limit_bytes=_vmem_limit(est)),
        name=name,
    )(*args)


def _cumsum_rows(x):
    row = lax.broadcasted_iota(jnp.int32, x.shape, 0)
    sh = 1
    while sh < x.shape[0]:
        x = x + jnp.where(row >= sh, pltpu.roll(x, sh, 0), 0.0)
        sh *= 2
    return x


def _stack_heads(x, m0, m1):
    return jnp.concatenate([x * m0, x * m1], axis=0)


def _nt_dot(a, b):
    return lax.dot_general(a.astype(BF16), b.astype(BF16), (((1,), (1,)), ((), ())), preferred_element_type=F32)


def _tn_dot(a, b):
    return lax.dot_general(a.astype(BF16), b.astype(BF16), (((0,), (0,)), ((), ())), preferred_element_type=F32)


def _nn_dot(a, b):
    return jnp.dot(a.astype(BF16), b.astype(BF16), preferred_element_type=F32)


def _rwkv_chunk(streams, hts, masks):
    pairs = range(len(streams))
    c = streams[0][0].shape[0]
    n2 = 2 * c
    m0, m1, strict, incl, eye, levels = masks
    ar, bk, v_st, p_last = [], [], [], []
    for r, ld, k, v, kk, b in streams:
        cs = _cumsum_rows(ld)
        p = jnp.exp(cs)
        p_inv = jnp.exp(-cs)
        a_st = _stack_heads(-kk * jnp.exp(cs - ld), m0, m1)
        r_st = _stack_heads(r * p, m0, m1)
        b_st = _stack_heads(b * p_inv, m0, m1)
        k_st = _stack_heads(k * p_inv, m0, m1)
        ar.append(jnp.concatenate([a_st, r_st], axis=0).astype(BF16))
        bk.append(jnp.concatenate([b_st, k_st], axis=0).astype(BF16))
        v_st.append(_stack_heads(v, m0, m1))
        p_last.append(p[c - 1:c, :])

    q = [_nt_dot(ar[i], bk[i]) for i in pairs]
    l_ab = [q[i][:n2, :n2] * strict for i in pairs]
    t = [eye + l_ab[i] * levels[0] for i in pairs]
    for lvl in levels[1:]:
        tc = [_nn_dot(t[i], l_ab[i] * lvl) for i in pairs]
        t = [t[i] + _nn_dot(tc[i], t[i]) for i in pairs]

    ar_h = [_nt_dot(ar[i], hts[i]) for i in pairs]
    ak_v = [_nn_dot(q[i][:n2, n2:] * strict, v_st[i]) for i in pairs]
    u_st = [_nn_dot(t[i], ar_h[i][:n2] + ak_v[i]) for i in pairs]
    uv = [jnp.concatenate([u_st[i], v_st[i]], axis=0).astype(BF16) for i in pairs]
    y_st = [ar_h[i][n2:] + _nn_dot(q[i][n2:, :] * jnp.concatenate([incl, incl], axis=1), uv[i]) for i in pairs]
    ys = [y_st[i][:c] + y_st[i][c:] for i in pairs]
    ht_new = [(hts[i] + _tn_dot(uv[i], bk[i])) * p_last[i] for i in pairs]
    return ys, ht_new


def _rwkv_scan_body(r_ref, ld_ref, k_ref, v_ref, kk_ref, b_ref, bonus_ref, g_ref, gw_ref, gb_ref, ones_ref,
                    o_ref, ht_ref):
    c = r_ref.shape[1]
    n2 = 2 * c
    lanes = V7X_LANES

    @pl.when(pl.program_id(1) == 0)
    def _():
        ht_ref[...] = jnp.zeros_like(ht_ref)

    lane = lax.broadcasted_iota(jnp.int32, (1, lanes), 1)
    m0 = jnp.where(lane < B_HEAD_DIM, 1.0, 0.0)
    m1 = 1.0 - m0
    row = lax.broadcasted_iota(jnp.int32, (n2, n2), 0)
    col = lax.broadcasted_iota(jnp.int32, (n2, n2), 1)
    log2c = c.bit_length() - 1
    same = jnp.right_shift(row, log2c) == jnp.right_shift(col, log2c)
    strict = jnp.where(same & (row > col), 1.0, 0.0)
    incl = jnp.where(same & (row >= col), 1.0, 0.0)
    eye = jnp.where(row == col, 1.0, 0.0)
    levels = []
    for lh in range(log2c):
        blk = jnp.right_shift(row, lh + 1) == jnp.right_shift(col, lh + 1)
        lower_left = (jnp.bitwise_and(jnp.right_shift(row, lh), 1) == 1) & \
                     (jnp.bitwise_and(jnp.right_shift(col, lh), 1) == 0)
        levels.append(jnp.where(blk & lower_left, 1.0, 0.0))
    masks = (m0, m1, strict, incl, eye, levels)

    n_pairs = r_ref.shape[2] // lanes
    cols = [slice(i * lanes, (i + 1) * lanes) for i in range(n_pairs)]
    streams = [tuple(ref[0, :, cs] for ref in (r_ref, ld_ref, k_ref, v_ref, kk_ref, b_ref)) for cs in cols]
    ys, ht_new = _rwkv_chunk(streams, [ht_ref[i] for i in range(n_pairs)], masks)
    for i in range(n_pairs):
        ht_ref[i] = ht_new[i]
    inv_n = 1.0 / B_HEAD_DIM
    means = [jnp.dot(y.astype(BF16), ones_ref[...], preferred_element_type=F32) * inv_n for y in ys]
    ds = [y - mean for y, mean in zip(ys, means)]
    vrs = [jnp.dot((d * d).astype(BF16), ones_ref[...], preferred_element_type=F32) * inv_n for d in ds]
    for cs, d, var in zip(cols, ds, vrs):
        yn = d * lax.rsqrt(var + GN_EPS) * gw_ref[:, cs] + gb_ref[:, cs]
        o_ref[0, :, cs] = ((yn + bonus_ref[0, :, cs]) * g_ref[0, :, cs]).astype(o_ref.dtype)


def rwkv_scan(r, ld, k, v, kk, b, bonus, g, gn_w, gn_b, name):
    bsz, s, bw = r.shape
    c = RWKV_CHUNK
    assert s % c == 0 and c & (c - 1) == 0 and 2 * B_HEAD_DIM == V7X_LANES
    tok = pl.BlockSpec((1, c, bw), lambda bi, i: (bi, i, 0))
    par = pl.BlockSpec((1, bw), lambda bi, i: (0, 0))
    est = (18 * _nbytes((c, bw), F32) + (bw // V7X_LANES) * _nbytes((V7X_LANES, V7X_LANES), F32)
           + 64 * _nbytes((4 * c, 4 * c), F32))
    return pl.pallas_call(
        _rwkv_scan_body,
        grid=(bsz, s // c),
        in_specs=[tok] * 8 + [par] * 2 + [pl.BlockSpec((V7X_LANES, V7X_LANES), lambda bi, i: (0, 0))],
        out_specs=tok,
        out_shape=jax.ShapeDtypeStruct((bsz, s, bw), BF16),
        scratch_shapes=[pltpu.VMEM((bw // V7X_LANES, V7X_LANES, V7X_LANES), F32)],
        compiler_params=pltpu.CompilerParams(
            dimension_semantics=("parallel", "arbitrary"), vmem_limit_bytes=_vmem_limit(est)),
        name=name,
    )(r, ld, k, v, kk, b, bonus, g, gn_w.reshape(1, bw), gn_b.reshape(1, bw), _head_pair_ones())


def _lora_proj_weight(w_in, has_vres):
    o = 3 * A_WIDTH + 3 * B_WIDTH
    d = w_in.shape[0]
    wz = jnp.zeros((d, Z_WIDTH), w_in.dtype)
    wz = wz.at[:, ZW_OFF:ZW_OFF + LORA_W].set(w_in[:, o:o + LORA_W])
    o += LORA_W
    wz = wz.at[:, ZA_OFF:ZA_OFF + LORA_A].set(w_in[:, o:o + LORA_A])
    o += LORA_A
    wz = wz.at[:, ZG_OFF:ZG_OFF + LORA_G].set(w_in[:, o:o + LORA_G])
    o += LORA_G
    if has_vres:
        wz = wz.at[:, ZV_OFF:ZV_OFF + LORA_V].set(w_in[:, o:o + LORA_V])
    return wz


def moba_rwkv_layer(x, b, s, li, norm_mix, w_in, q_gain, k_gain, shift_mu, w0, w_lora, a0, a_lora, g_lora,
                    k_k, k_a, r_k, gn_w, gn_b, w_out, v0, v_lora, v_first):
    has_vres = v_lora is not None
    tag = "l%d_" % li
    n_main = 3 * A_WIDTH + 3 * B_WIDTH
    proj = norm_matmul(x, norm_mix, w_in[:, :n_main].astype(BF16), F32, tag + "in_proj")
    pz = norm_matmul(x, norm_mix, _lora_proj_weight(w_in, has_vres).astype(BF16), F32, tag + "lora_proj")
    proj = proj.reshape(b, s, n_main)
    pz = pz.reshape(b, s, Z_WIDTH)

    y_a = moba_attention(proj, q_gain, k_gain, tag + "moba")

    r, ld, k, v, kk, kb, bonus, g = rwkv_prep(proj, pz, shift_mu, w0, w_lora, a0, a_lora, g_lora, k_k, k_a,
                                              r_k.reshape(-1), v0, v_lora, v_first, tag + "rwkv_prep")
    if not has_vres:
        v_first = v
    y_b = rwkv_scan(r, ld, k, v, kk, kb, bonus, g, gn_w, gn_b, tag + "rwkv_scan")

    x = mix_out_proj(y_a.reshape(b * s, A_WIDTH), y_b.reshape(b * s, B_WIDTH), w_out.astype(BF16), x,
                     tag + "out_proj")
    return x, v_first


def short_conv_layer(x, b, s, li, norm_mix, conv_in, conv_w, conv_out):
    tag = "l%d_" % li
    m = norm_matmul_conv(x, norm_mix, conv_in.astype(BF16), conv_w, s, 3, _short_conv_combine, tag + "conv_in")
    return matmul_residual(m, conv_out.astype(BF16), x, tag + "conv_out")


def conv_ffn(x, b, s, li, norm_ffn, ffn_up, ffn_conv, ffn_down):
    tag = "l%d_" % li
    m = norm_matmul_conv(x, norm_ffn, ffn_up.astype(BF16), ffn_conv, s, 2, _ffn_combine, tag + "ffn_up")
    return matmul_residual(m, ffn_down.astype(BF16), x, tag + "ffn_down")


def kernel(x, l0_norm_mix, l0_w_in, l0_q_gain, l0_k_gain, l0_shift_mu, l0_w0, l0_w_lora, l0_a0, l0_a_lora, l0_g_lora, l0_k_k, l0_k_a, l0_r_k, l0_gn_w, l0_gn_b, l0_w_out, l0_norm_ffn, l0_ffn_up, l0_ffn_conv, l0_ffn_down, l1_norm_mix, l1_conv_in, l1_conv_w, l1_conv_out, l1_norm_ffn, l1_ffn_up, l1_ffn_conv, l1_ffn_down, l2_norm_mix, l2_w_in, l2_q_gain, l2_k_gain, l2_shift_mu, l2_w0, l2_w_lora, l2_a0, l2_a_lora, l2_g_lora, l2_k_k, l2_k_a, l2_r_k, l2_gn_w, l2_gn_b, l2_w_out, l2_v0, l2_v_lora, l2_norm_ffn, l2_ffn_up, l2_ffn_conv, l2_ffn_down, l3_norm_mix, l3_conv_in, l3_conv_w, l3_conv_out, l3_norm_ffn, l3_ffn_up, l3_ffn_conv, l3_ffn_down):
    b, s, d = x.shape
    h = x.reshape(b * s, d)

    h, v_first = moba_rwkv_layer(
        h, b, s, 0, l0_norm_mix, l0_w_in, l0_q_gain, l0_k_gain, l0_shift_mu, l0_w0, l0_w_lora, l0_a0,
        l0_a_lora, l0_g_lora, l0_k_k, l0_k_a, l0_r_k, l0_gn_w, l0_gn_b, l0_w_out, None, None, None)
    h = conv_ffn(h, b, s, 0, l0_norm_ffn, l0_ffn_up, l0_ffn_conv, l0_ffn_down)

    h = short_conv_layer(h, b, s, 1, l1_norm_mix, l1_conv_in, l1_conv_w, l1_conv_out)
    h = conv_ffn(h, b, s, 1, l1_norm_ffn, l1_ffn_up, l1_ffn_conv, l1_ffn_down)

    h, _ = moba_rwkv_layer(
        h, b, s, 2, l2_norm_mix, l2_w_in, l2_q_gain, l2_k_gain, l2_shift_mu, l2_w0, l2_w_lora, l2_a0,
        l2_a_lora, l2_g_lora, l2_k_k, l2_k_a, l2_r_k, l2_gn_w, l2_gn_b, l2_w_out, l2_v0, l2_v_lora, v_first)
    h = conv_ffn(h, b, s, 2, l2_norm_ffn, l2_ffn_up, l2_ffn_conv, l2_ffn_down)

    h = short_conv_layer(h, b, s, 3, l3_norm_mix, l3_conv_in, l3_conv_w, l3_conv_out)
    h = conv_ffn(h, b, s, 3, l3_norm_ffn, l3_ffn_up, l3_ffn_conv, l3_ffn_down)
    return h.reshape(b, s, d)
```

```python
import functools

import jax
import jax.numpy as jnp
from jax import lax
from jax.experimental import pallas as pl
from jax.experimental.pallas import tpu as pltpu

F32 = jnp.float32
BF16 = jnp.bfloat16

A_HEADS = 8
A_HEAD_DIM = 128
A_WIDTH = A_HEADS * A_HEAD_DIM
MOBA_BLOCK = 256
MOBA_TOPK = 3
MOBA_HEADS_PER_STEP = 2
MOBA_DENOM_ROWS = 16
B_HEADS = 16
B_HEAD_DIM = 64
B_WIDTH = B_HEADS * B_HEAD_DIM
RWKV_CHUNK = 64
RWKV_SEQS_PER_STEP = 2
LORA_W = 64
LORA_A = 64
LORA_G = 160
LORA_V = 32
CONV_WIDTH = 3
RMS_EPS = 1e-6
GN_EPS = 64e-5
NEG_INF = -1e30
LOG2E = 1.4426950408889634

V7X_LANES = 128
V7X_SUBLANES = 8
V7X_VMEM_BYTES = 64 * 1024 * 1024
V7X_VMEM_CAP = V7X_VMEM_BYTES - 8 * 1024 * 1024
SPILL_ALLOWANCE = 4 * 1024 * 1024
CONV_ROW_CHUNK = 256

ZW_OFF, ZW_PAD = 0, 128
ZA_OFF, ZA_PAD = 128, 128
ZG_OFF, ZG_PAD = 256, 256
ZV_OFF, ZV_PAD = 512, 128
Z_WIDTH = 640


def _vmem_limit(nbytes):
    return int(min(max(nbytes + SPILL_ALLOWANCE, 16 * 1024 * 1024), V7X_VMEM_CAP))


def _nbytes(shape, dtype):
    n = 1
    for s in shape:
        n *= s
    return n * jnp.dtype(dtype).itemsize


def _pick(n, prefs):
    for p in prefs:
        if n % p == 0:
            return p
    return n


def _norm_matmul_body(x_ref, g_ref, w_ref, o_ref, xn_ref):
    @pl.when(pl.program_id(1) == 0)
    def _():
        x = x_ref[...]
        ms = jnp.mean(x * x, axis=-1, keepdims=True)
        xn_ref[...] = (x * lax.rsqrt(ms + RMS_EPS) * g_ref[...]).astype(BF16)

    o_ref[...] = jnp.dot(xn_ref[...], w_ref[...], preferred_element_type=F32).astype(o_ref.dtype)


def norm_matmul(x, gain, w, out_dtype, name):
    m, k = x.shape
    n = w.shape[1]
    tm = _pick(m, (1024, 512, 256, 128))
    tn = _pick(n, (1024, 768, 640, 512, 256, 128))
    est = (2 * _nbytes((tm, k), F32) + _nbytes((tm, k), BF16) + 2 * _nbytes((k, tn), BF16)
           + 2 * _nbytes((tm, tn), out_dtype) + 2 * _nbytes((tm, tn), F32) + _nbytes((tm, k), F32))
    return pl.pallas_call(
        _norm_matmul_body,
        grid=(m // tm, n // tn),
        in_specs=[
            pl.BlockSpec((tm, k), lambda i, j: (i, 0)),
            pl.BlockSpec((1, k), lambda i, j: (0, 0)),
            pl.BlockSpec((k, tn), lambda i, j: (0, j)),
        ],
        out_specs=pl.BlockSpec((tm, tn), lambda i, j: (i, j)),
        out_shape=jax.ShapeDtypeStruct((m, n), out_dtype),
        scratch_shapes=[pltpu.VMEM((tm, k), BF16)],
        compiler_params=pltpu.CompilerParams(
            dimension_semantics=("parallel", "arbitrary"), vmem_limit_bytes=_vmem_limit(est)),
        name=name,
    )(x, gain.reshape(1, k), w)


def _norm_matmul_split_body(x_ref, g_ref, w_ref, oa_ref, ob_ref, xn_ref, *, na):
    j = pl.program_id(1)

    @pl.when(j == 0)
    def _():
        x = x_ref[...]
        ms = jnp.mean(x * x, axis=-1, keepdims=True)
        xn_ref[...] = (x * lax.rsqrt(ms + RMS_EPS) * g_ref[...]).astype(BF16)

    h = jnp.dot(xn_ref[...], w_ref[...], preferred_element_type=F32)

    @pl.when(j < na)
    def _():
        oa_ref[...] = h.astype(oa_ref.dtype)

    @pl.when(j >= na)
    def _():
        ob_ref[...] = h.astype(ob_ref.dtype)


def norm_matmul_split(x, gain, w, n_a, dtype_a, dtype_b, name):
    m, k = x.shape
    n = w.shape[1]
    tm = _pick(m, (1024, 512, 256, 128))
    tn = _pick(n_a, (1024, 512, 256, 128))
    assert n % tn == 0
    na = n_a // tn
    est = (2 * _nbytes((tm, k), F32) + _nbytes((tm, k), BF16) + 2 * _nbytes((k, tn), BF16)
           + 8 * _nbytes((tm, tn), F32) + _nbytes((tm, k), F32))
    return pl.pallas_call(
        functools.partial(_norm_matmul_split_body, na=na),
        grid=(m // tm, n // tn),
        in_specs=[
            pl.BlockSpec((tm, k), lambda i, j: (i, 0)),
            pl.BlockSpec((1, k), lambda i, j: (0, 0)),
            pl.BlockSpec((k, tn), lambda i, j: (0, j)),
        ],
        out_specs=[
            pl.BlockSpec((tm, tn), lambda i, j: (i, jnp.minimum(j, na - 1))),
            pl.BlockSpec((tm, tn), lambda i, j: (i, jnp.maximum(j - na, 0))),
        ],
        out_shape=[jax.ShapeDtypeStruct((m, n_a), dtype_a), jax.ShapeDtypeStruct((m, n - n_a), dtype_b)],
        scratch_shapes=[pltpu.VMEM((tm, k), BF16)],
        compiler_params=pltpu.CompilerParams(
            dimension_semantics=("arbitrary", "arbitrary"), vmem_limit_bytes=_vmem_limit(est)),
        name=name,
    )(x, gain.reshape(1, k), w)


def _conv3_ext(u, w):
    return w[0:1, :] * pltpu.roll(u, 2, 0) + w[1:2, :] * pltpu.roll(u, 1, 0) + w[2:3, :] * u


def _norm_matmul_conv_body(x_ref, g_ref, *refs, n_groups, n_taps, tiles_per_seq, combine):
    w_refs = refs[:n_groups]
    cw_refs = refs[n_groups:n_groups + n_taps]
    o_ref, xn_ref, carry_ref = refs[n_groups + n_taps:]
    i = pl.program_id(0)
    j = pl.program_id(1)
    tm, tn = o_ref.shape
    sub = V7X_SUBLANES
    rc = min(tm, CONV_ROW_CHUNK)

    @pl.when(j == 0)
    def _():
        x = x_ref[...]
        ms = jnp.mean(x * x, axis=-1, keepdims=True)
        xn_ref[...] = (x * lax.rsqrt(ms + RMS_EPS) * g_ref[...]).astype(BF16)

    @pl.when(i % tiles_per_seq == 0)
    def _():
        carry_ref[j] = jnp.zeros(carry_ref.shape[1:], F32)

    w = jnp.concatenate([w_ref[...] for w_ref in w_refs], axis=1)
    h = jnp.dot(xn_ref[...], w, preferred_element_type=F32)
    taps = [cw_ref[...] for cw_ref in cw_refs]
    for c in range(tm // rc):
        r0 = c * rc
        if c == 0:
            ext = jnp.concatenate([carry_ref[j], h[:rc, :]], axis=0)
        else:
            ext = h[r0 - sub:r0 + rc, :]
        groups = [ext[:, g * tn:(g + 1) * tn] for g in range(n_groups)]
        o_ref[r0:r0 + rc, :] = combine(groups, taps)[sub:, :].astype(o_ref.dtype)
    carry_ref[j] = h[tm - sub:, :]


def _ffn_combine(groups, taps):
    return jax.nn.silu(_conv3_ext(groups[0], taps[0])) * _conv3_ext(groups[1], taps[1])


def _short_conv_combine(groups, taps):
    return groups[0] * _conv3_ext(groups[1] * groups[2], taps[0])


def norm_matmul_conv(x, gain, w, conv_w, seq, n_groups, combine, name):
    m, k = x.shape
    n = w.shape[1] // n_groups
    tm = _pick(seq, (1024, 512, 256, 128))
    tn = _pick(n, (512, 256, 128))
    nj = n // tn
    n_cw = conv_w.shape[1] // n
    est = (2 * _nbytes((tm, k), F32) + _nbytes((tm, k), BF16) + 3 * n_groups * _nbytes((k, tn), BF16)
           + 2 * _nbytes((tm, tn), BF16) + (n_groups + 2) * _nbytes((tm, tn), F32)
           + 8 * n_groups * _nbytes((CONV_ROW_CHUNK, tn), F32) + nj * n_groups * _nbytes((V7X_SUBLANES, tn), F32))
    w_specs = [pl.BlockSpec((k, tn), lambda i, j, g=g: (0, g * nj + j)) for g in range(n_groups)]
    cw_specs = [pl.BlockSpec((CONV_WIDTH, tn), lambda i, j, g=g: (0, g * nj + j)) for g in range(n_cw)]
    return pl.pallas_call(
        functools.partial(_norm_matmul_conv_body, n_groups=n_groups, n_taps=n_cw, tiles_per_seq=seq // tm,
                          combine=combine),
        grid=(m // tm, nj),
        in_specs=[pl.BlockSpec((tm, k), lambda i, j: (i, 0)), pl.BlockSpec((1, k), lambda i, j: (0, 0))]
        + w_specs + cw_specs,
        out_specs=pl.BlockSpec((tm, tn), lambda i, j: (i, j)),
        out_shape=jax.ShapeDtypeStruct((m, n), BF16),
        scratch_shapes=[pltpu.VMEM((tm, k), BF16),
                        pltpu.VMEM((nj, V7X_SUBLANES, n_groups * tn), F32)],
        compiler_params=pltpu.CompilerParams(
            dimension_semantics=("arbitrary", "arbitrary"), vmem_limit_bytes=_vmem_limit(est)),
        name=name,
    )(x, gain.reshape(1, k), *([w] * n_groups), *([conv_w] * n_cw))


def _matmul_res_body(a_ref, w_ref, r_ref, o_ref):
    o_ref[...] = r_ref[...] + jnp.dot(a_ref[...], w_ref[...], preferred_element_type=F32)


def matmul_residual(a, w, res, name):
    m, k = a.shape
    n = w.shape[1]
    tm = _pick(m, (1024, 512, 256, 128))
    tn = _pick(n, (512, 256, 128)) if k > 4096 else _pick(n, (1024, 512, 256, 128))
    est = (2 * _nbytes((tm, k), BF16) + 2 * _nbytes((k, tn), BF16) + 6 * _nbytes((tm, tn), F32))
    return pl.pallas_call(
        _matmul_res_body,
        grid=(m // tm, n // tn),
        in_specs=[
            pl.BlockSpec((tm, k), lambda i, j: (i, 0)),
            pl.BlockSpec((k, tn), lambda i, j: (0, j)),
            pl.BlockSpec((tm, tn), lambda i, j: (i, j)),
        ],
        out_specs=pl.BlockSpec((tm, tn), lambda i, j: (i, j)),
        out_shape=jax.ShapeDtypeStruct((m, n), F32),
        compiler_params=pltpu.CompilerParams(
            dimension_semantics=("parallel", "parallel"), vmem_limit_bytes=_vmem_limit(est)),
        name=name,
    )(a, w, res)


def _mix_out_body(ya_ref, yb_ref, wa_ref, wb_ref, r_ref, o_ref):
    acc = jnp.dot(ya_ref[...], wa_ref[...], preferred_element_type=F32)
    acc = acc + jnp.dot(yb_ref[...], wb_ref[...], preferred_element_type=F32)
    o_ref[...] = r_ref[...] + acc


def mix_out_proj(ya, yb, w_out, res, name):
    m, ka = ya.shape
    kb = yb.shape[1]
    n = w_out.shape[1]
    tm = _pick(m, (1024, 512, 256, 128))
    tn = _pick(n, (1024, 512, 256, 128))
    est = 2 * _nbytes((tm, ka + kb), BF16) + 2 * _nbytes((ka + kb, tn), BF16) + 6 * _nbytes((tm, tn), F32)
    return pl.pallas_call(
        _mix_out_body,
        grid=(m // tm, n // tn),
        in_specs=[
            pl.BlockSpec((tm, ka), lambda i, j: (i, 0)),
            pl.BlockSpec((tm, kb), lambda i, j: (i, 0)),
            pl.BlockSpec((ka, tn), lambda i, j: (0, j)),
            pl.BlockSpec((kb, tn), lambda i, j: (0, j)),
            pl.BlockSpec((tm, tn), lambda i, j: (i, j)),
        ],
        out_specs=pl.BlockSpec((tm, tn), lambda i, j: (i, j)),
        out_shape=jax.ShapeDtypeStruct((m, n), F32),
        compiler_params=pltpu.CompilerParams(
            dimension_semantics=("parallel", "parallel"), vmem_limit_bytes=_vmem_limit(est)),
        name=name,
    )(ya, yb, w_out[:ka], w_out[ka:], res)


def _head_rms(x, gain):
    return x * lax.rsqrt(jnp.mean(x * x, axis=-1, keepdims=True) + RMS_EPS) * gain


def _moba_query_block(ii, q, q_gain, slope, rel_t, kn_ref, vt_ref, km_ref):
    bs = MOBA_BLOCK
    hd = A_HEAD_DIM
    qn = _head_rms(q, q_gain)
    qs = (qn * (hd ** -0.5 * LOG2E)).astype(BF16)
    slope2 = slope * LOG2E
    bias_rel = slope2 * rel_t
    gt = lax.dot_general(km_ref[...], qn, (((1,), (1,)), ((), ())),
                         precision=lax.Precision.HIGHEST, preferred_element_type=F32)
    blk = lax.broadcasted_iota(jnp.int32, gt.shape, 0)
    gm = jnp.where(blk < ii, gt, NEG_INF)
    scores = []
    m_all = None
    for n in range(ii + 1):
        s = lax.dot_general(kn_ref[n * bs:(n + 1) * bs, :], qs, (((1,), (1,)), ((), ())),
                            preferred_element_type=F32) - bias_rel
        if n == ii:
            s = jnp.where(rel_t >= 0, s, NEG_INF)
            shift = jnp.zeros((1, bs), F32)
        else:
            g_n = gm[n:n + 1, :]
            beats = jnp.where((gm > g_n) | ((gm == g_n) & (blk < n)), 1.0, 0.0)
            chosen = jnp.sum(beats, axis=0, keepdims=True) < float(MOBA_TOPK)
            shift = jnp.where(chosen, slope2 * float((ii - n) * bs), -NEG_INF)
        m_blk = jnp.max(s, axis=0, keepdims=True) - shift
        m_all = m_blk if m_all is None else jnp.maximum(m_all, m_blk)
        scores.append((s, shift))
    acc = None
    for n, (s, shift) in enumerate(scores):
        p = jnp.exp2(s - (m_all + shift)).astype(BF16)
        pv = jnp.dot(vt_ref[:, n * bs:(n + 1) * bs], p, preferred_element_type=F32)
        acc = pv if acc is None else acc + pv
    return (acc[:hd] / acc[hd:hd + 1]).T


def _moba_body(qlo_ref, qhi_ref, k_ref, v_ref, qg_ref, kg_ref, olo_ref, ohi_ref, kn_ref, vt_ref, km_ref,
               *, n_heads, n_blocks):
    bs = MOBA_BLOCK
    hd = A_HEAD_DIM
    hp = pl.program_id(1)
    j = pl.program_id(2)

    @pl.when(j == 0)
    def _():
        km_ref[...] = jnp.zeros_like(km_ref)
        for e in range(MOBA_HEADS_PER_STEP):
            cols = slice(e * hd, (e + 1) * hd)
            for nb in range(n_blocks):
                rows = slice(nb * bs, (nb + 1) * bs)
                kn = _head_rms(k_ref[0, rows, cols], kg_ref[...])
                kn_ref[e, rows, :] = kn.astype(BF16)
                km_ref[e, nb:nb + 1, :] = jnp.mean(kn, axis=0, keepdims=True)
                vt_ref[e, :hd, rows] = v_ref[0, rows, cols].T.astype(BF16)
            vt_ref[e, hd:, :] = jnp.ones((MOBA_DENOM_ROWS, vt_ref.shape[2]), BF16)

    rel_t = (lax.broadcasted_iota(jnp.int32, (bs, bs), 1)
             - lax.broadcasted_iota(jnp.int32, (bs, bs), 0)).astype(F32)

    for jj in range(n_blocks // 2):
        @pl.when(j == jj)
        def _(jj=jj):
            for e in range(MOBA_HEADS_PER_STEP):
                cols = slice(e * hd, (e + 1) * hd)
                head = (hp * MOBA_HEADS_PER_STEP + e + 1).astype(F32)
                slope = jnp.exp2(jnp.full((1, bs), -8.0 / n_heads, F32) * head)
                for q_ref, o_ref, ii in ((qlo_ref, olo_ref, jj), (qhi_ref, ohi_ref, n_blocks - 1 - jj)):
                    o = _moba_query_block(ii, q_ref[0, :, cols], qg_ref[...], slope, rel_t,
                                          kn_ref.at[e], vt_ref.at[e], km_ref.at[e])
                    o_ref[0, :, cols] = o.astype(o_ref.dtype)


def moba_attention(proj, q_gain, k_gain, name):
    b, s, _ = proj.shape
    assert s % MOBA_BLOCK == 0
    nb = s // MOBA_BLOCK
    assert nb % 2 == 0 and A_HEADS % MOBA_HEADS_PER_STEP == 0
    nbp = -(-nb // V7X_SUBLANES) * V7X_SUBLANES
    hd = A_HEAD_DIM
    hw = MOBA_HEADS_PER_STEP * hd
    groups = A_HEADS // MOBA_HEADS_PER_STEP
    half = nb // 2
    est = (8 * _nbytes((MOBA_BLOCK, hw), F32) + 4 * _nbytes((s, hw), F32) + 2 * _nbytes((s, hw), BF16)
           + MOBA_HEADS_PER_STEP * (4 * nb + 16) * _nbytes((MOBA_BLOCK, MOBA_BLOCK), F32))
    o_lo, o_hi = pl.pallas_call(
        functools.partial(_moba_body, n_heads=A_HEADS, n_blocks=nb),
        grid=(b, groups, half),
        in_specs=[
            pl.BlockSpec((1, MOBA_BLOCK, hw), lambda bi, g, j: (bi, j, g)),
            pl.BlockSpec((1, MOBA_BLOCK, hw), lambda bi, g, j: (bi, nb - 1 - j, g)),
            pl.BlockSpec((1, s, hw), lambda bi, g, j: (bi, 0, groups + g)),
            pl.BlockSpec((1, s, hw), lambda bi, g, j: (bi, 0, 2 * groups + g)),
            pl.BlockSpec((1, hd), lambda bi, g, j: (0, 0)),
            pl.BlockSpec((1, hd), lambda bi, g, j: (0, 0)),
        ],
        out_specs=[
            pl.BlockSpec((1, MOBA_BLOCK, hw), lambda bi, g, j: (bi, j, g)),
            pl.BlockSpec((1, MOBA_BLOCK, hw), lambda bi, g, j: (bi, half - 1 - j, g)),
        ],
        out_shape=[jax.ShapeDtypeStruct((b, s // 2, A_WIDTH), BF16)] * 2,
        scratch_shapes=[
            pltpu.VMEM((MOBA_HEADS_PER_STEP, s, hd), BF16),
            pltpu.VMEM((MOBA_HEADS_PER_STEP, hd + MOBA_DENOM_ROWS, s), BF16),
            pltpu.VMEM((MOBA_HEADS_PER_STEP, nbp, hd), F32),
        ],
        compiler_params=pltpu.CompilerParams(
            dimension_semantics=("parallel", "parallel", "arbitrary"), vmem_limit_bytes=_vmem_limit(est)),
        name=name,
    )(proj, proj, proj, proj, q_gain.reshape(1, hd), k_gain.reshape(1, hd))
    return jnp.concatenate([o_lo, o_hi], axis=1)


def _token_shift(p, prev_last, mu):
    rolled = pltpu.roll(p, 1, 0)
    row = lax.broadcasted_iota(jnp.int32, p.shape, 0)
    p_prev = jnp.where(row == 0, prev_last, rolled)
    return p + mu * (p_prev - p)


def _head_sums(x, ones_ref):
    tiles = [jnp.dot(x[:, c:c + V7X_LANES].astype(BF16), ones_ref[...], preferred_element_type=F32)
             for c in range(0, x.shape[1], V7X_LANES)]
    return jnp.concatenate(tiles, axis=1)


def _rwkv_prep_body(*refs, has_vres):
    if has_vres:
        (pm_ref, pz_ref, mum_ref, muz_ref, w0_ref, wl_ref, a0_ref, al_ref, gl_ref, kk_ref, ka_ref, rk_ref,
         ones_ref, v0_ref, vl_ref, vf_ref,
         r_ref, ld_ref, k_ref, v_ref, kn_ref, b_ref, bonus_ref, g_ref, cm_ref, cz_ref) = refs
    else:
        (pm_ref, pz_ref, mum_ref, muz_ref, w0_ref, wl_ref, a0_ref, al_ref, gl_ref, kk_ref, ka_ref, rk_ref,
         ones_ref,
         r_ref, ld_ref, k_ref, v_ref, kn_ref, b_ref, bonus_ref, g_ref, cm_ref, cz_ref) = refs
    tm = pm_ref.shape[1]
    bw = B_WIDTH

    @pl.when(pl.program_id(1) == 0)
    def _():
        cm_ref[...] = jnp.zeros_like(cm_ref)
        cz_ref[...] = jnp.zeros_like(cz_ref)

    pm = pm_ref[0].astype(F32)
    pz = pz_ref[0]
    last = V7X_SUBLANES - 1
    pms = _token_shift(pm, cm_ref[last:last + 1, :], mum_ref[...])
    pzs = _token_shift(pz, cz_ref[last:last + 1, :], muz_ref[...])
    cm_ref[...] = pm[tm - V7X_SUBLANES:, :]
    cz_ref[...] = pz[tm - V7X_SUBLANES:, :]

    r = pms[:, :bw]
    k = pms[:, bw:2 * bw]
    v = pms[:, 2 * bw:3 * bw]

    z_w = jnp.tanh(pzs[:, ZW_OFF:ZW_OFF + ZW_PAD]).astype(BF16)
    w = w0_ref[...] + jnp.dot(z_w, wl_ref[...], preferred_element_type=F32)
    w = -jax.nn.softplus(-w) - 0.5
    ld_ref[0] = -jnp.exp(w)

    z_a = pzs[:, ZA_OFF:ZA_OFF + ZA_PAD].astype(BF16)
    a = jax.nn.sigmoid(a0_ref[...] + jnp.dot(z_a, al_ref[...], preferred_element_type=F32))

    z_g = jax.nn.sigmoid(pzs[:, ZG_OFF:ZG_OFF + ZG_PAD]).astype(BF16)
    g_ref[0] = jnp.dot(z_g, gl_ref[...], preferred_element_type=F32).astype(g_ref.dtype)

    if has_vres:
        z_v = pzs[:, ZV_OFF:ZV_OFF + ZV_PAD].astype(BF16)
        mix = jax.nn.sigmoid(v0_ref[...] + jnp.dot(z_v, vl_ref[...], preferred_element_type=F32))
        v = v + (vf_ref[0].astype(F32) - v) * mix

    kk = k * kk_ref[...]
    kk = kk * jnp.minimum(lax.rsqrt(_head_sums(kk * kk, ones_ref)), 1e12)
    k = k * (1.0 + (a - 1.0) * ka_ref[...])
    r_ref[0] = r.astype(r_ref.dtype)
    k_ref[0] = k.astype(k_ref.dtype)
    v_ref[0] = v.astype(v_ref.dtype)
    kn_ref[0] = kk.astype(kn_ref.dtype)
    b_ref[0] = (kk * a).astype(b_ref.dtype)
    bonus_ref[0] = (_head_sums(r * k * rk_ref[...], ones_ref) * v).astype(bonus_ref.dtype)


def _pad_rows(w, rows):
    return jnp.pad(w, ((0, rows - w.shape[0]), (0, 0)))


def _head_pair_ones():
    head = jnp.arange(V7X_LANES) // B_HEAD_DIM
    return (head[:, None] == head[None, :]).astype(BF16)


def rwkv_prep(proj, pz, shift_mu, w0, w_lora, a0, a_lora, g_lora, k_k, k_a, r_k, v0, v_lora, v_first, name):
    b, s, main_blk = proj.shape
    bw = B_WIDTH
    has_vres = v_lora is not None
    tm = _pick(s, (256, 128))
    assert main_blk == 3 * bw

    mu_main = shift_mu[:3 * bw].reshape(1, 3 * bw)
    lora_mu = shift_mu[3 * bw:]
    mu_z = jnp.zeros((Z_WIDTH,), F32)
    mu_z = mu_z.at[ZW_OFF:ZW_OFF + LORA_W].set(lora_mu[:LORA_W])
    mu_z = mu_z.at[ZA_OFF:ZA_OFF + LORA_A].set(lora_mu[LORA_W:LORA_W + LORA_A])
    mu_z = mu_z.at[ZG_OFF:ZG_OFF + LORA_G].set(lora_mu[LORA_W + LORA_A:LORA_W + LORA_A + LORA_G])
    if has_vres:
        mu_z = mu_z.at[ZV_OFF:ZV_OFF + LORA_V].set(lora_mu[LORA_W + LORA_A + LORA_G:])
    mu_z = mu_z.reshape(1, Z_WIDTH)

    row = lambda t: t.reshape(1, bw)
    const = lambda shape: pl.BlockSpec(shape, lambda bi, i: (0, 0))
    tok = pl.BlockSpec((1, tm, bw), lambda bi, i: (bi, i, 0))
    args = [proj, pz, mu_main, mu_z, row(w0), _pad_rows(w_lora, ZW_PAD).astype(BF16),
            row(a0), _pad_rows(a_lora, ZA_PAD).astype(BF16), _pad_rows(g_lora, ZG_PAD).astype(BF16),
            row(k_k), row(k_a), row(r_k), _head_pair_ones()]
    in_specs = [
        pl.BlockSpec((1, tm, main_blk), lambda bi, i: (bi, i, 0)),
        pl.BlockSpec((1, tm, Z_WIDTH), lambda bi, i: (bi, i, 0)),
        const((1, main_blk)), const((1, Z_WIDTH)),
        const((1, bw)), const((ZW_PAD, bw)), const((1, bw)), const((ZA_PAD, bw)), const((ZG_PAD, bw)),
        const((1, bw)), const((1, bw)), const((1, bw)), const((V7X_LANES, V7X_LANES)),
    ]
    if has_vres:
        args += [row(v0), _pad_rows(v_lora, ZV_PAD).astype(BF16), v_first]
        in_specs += [const((1, bw)), const((ZV_PAD, bw)), tok]
    out_dtypes = [BF16, F32, BF16, BF16, BF16, BF16, BF16, BF16]
    n_out = len(out_dtypes)
    est = (2 * _nbytes((tm, main_blk + Z_WIDTH), F32) + (2 * n_out + 6) * _nbytes((tm, bw), F32)
           + 6 * _nbytes((tm, main_blk), F32) + 4 * _nbytes((ZG_PAD + 3 * ZW_PAD, bw), BF16))
    return pl.pallas_call(
        functools.partial(_rwkv_prep_body, has_vres=has_vres),
        grid=(b, s // tm),
        in_specs=in_specs,
        out_specs=[tok] * n_out,
        out_shape=[jax.ShapeDtypeStruct((b, s, bw), dt) for dt in out_dtypes],
        scratch_shapes=[pltpu.VMEM((V7X_SUBLANES, main_blk), F32), pltpu.VMEM((V7X_SUBLANES, Z_WIDTH), F32)],
        compiler_params=pltpu.CompilerParams(
            dimension_semantics=("parallel", "arbitrary"), vmem_limit_bytes=_vmem_limit(est)),
        name=name,
    )(*args)


def _cumsum_rows(x):
    row = lax.broadcasted_iota(jnp.int32, x.shape, 0)
    sh = 1
    while sh < x.shape[0]:
        x = x + jnp.where(row >= sh, pltpu.roll(x, sh, 0), 0.0)
        sh *= 2
    return x


def _stack_heads(x, m0, m1):
    return jnp.concatenate([x * m0, x * m1], axis=0)


def _nt_dot(a, b):
    return lax.dot_general(a.astype(BF16), b.astype(BF16), (((1,), (1,)), ((), ())), preferred_element_type=F32)


def _tn_dot(a, b):
    return lax.dot_general(a.astype(BF16), b.astype(BF16), (((0,), (0,)), ((), ())), preferred_element_type=F32)


def _nn_dot(a, b):
    return jnp.dot(a.astype(BF16), b.astype(BF16), preferred_element_type=F32)


def _rwkv_chunk(streams, hts, masks):
    pairs = range(len(streams))
    c = streams[0][0].shape[0]
    n2 = 2 * c
    m0, m1, strict, incl, eye, levels = masks
    ar, bk, v_st, p_last = [], [], [], []
    for r, ld, k, v, kk, b in streams:
        cs = _cumsum_rows(ld)
        p = jnp.exp(cs)
        p_inv = jnp.exp(-cs)
        a_st = _stack_heads(-kk * jnp.exp(cs - ld), m0, m1)
        r_st = _stack_heads(r * p, m0, m1)
        b_st = _stack_heads(b * p_inv, m0, m1)
        k_st = _stack_heads(k * p_inv, m0, m1)
        ar.append(jnp.concatenate([a_st, r_st], axis=0).astype(BF16))
        bk.append(jnp.concatenate([b_st, k_st], axis=0).astype(BF16))
        v_st.append(_stack_heads(v, m0, m1))
        p_last.append(p[c - 1:c, :])

    q = [_nt_dot(ar[i], bk[i]) for i in pairs]
    l_ab = [q[i][:n2, :n2] * strict for i in pairs]
    t = [eye + l_ab[i] * levels[0] for i in pairs]
    for lvl in levels[1:]:
        tc = [_nn_dot(t[i], l_ab[i] * lvl) for i in pairs]
        t = [t[i] + _nn_dot(tc[i], t[i]) for i in pairs]

    ar_h = [_nt_dot(ar[i], hts[i]) for i in pairs]
    ak_v = [_nn_dot(q[i][:n2, n2:] * strict, v_st[i]) for i in pairs]
    u_st = [_nn_dot(t[i], ar_h[i][:n2] + ak_v[i]) for i in pairs]
    uv = [jnp.concatenate([u_st[i], v_st[i]], axis=0).astype(BF16) for i in pairs]
    y_st = [ar_h[i][n2:] + _nn_dot(q[i][n2:, :] * jnp.concatenate([incl, incl], axis=1), uv[i]) for i in pairs]
    ys = [y_st[i][:c] + y_st[i][c:] for i in pairs]
    ht_new = [(hts[i] + _tn_dot(uv[i], bk[i])) * p_last[i] for i in pairs]
    return ys, ht_new


def _rwkv_scan_body(r_ref, ld_ref, k_ref, v_ref, kk_ref, b_ref, bonus_ref, g_ref, gw_ref, gb_ref, ones_ref,
                    o_ref, ht_ref):
    c = r_ref.shape[1]
    n2 = 2 * c
    lanes = V7X_LANES

    @pl.when(pl.program_id(1) == 0)
    def _():
        ht_ref[...] = jnp.zeros_like(ht_ref)

    lane = lax.broadcasted_iota(jnp.int32, (1, lanes), 1)
    m0 = jnp.where(lane < B_HEAD_DIM, 1.0, 0.0)
    m1 = 1.0 - m0
    row = lax.broadcasted_iota(jnp.int32, (n2, n2), 0)
    col = lax.broadcasted_iota(jnp.int32, (n2, n2), 1)
    log2c = c.bit_length() - 1
    same = jnp.right_shift(row, log2c) == jnp.right_shift(col, log2c)
    strict = jnp.where(same & (row > col), 1.0, 0.0)
    incl = jnp.where(same & (row >= col), 1.0, 0.0)
    eye = jnp.where(row == col, 1.0, 0.0)
    levels = []
    for lh in range(log2c):
        blk = jnp.right_shift(row, lh + 1) == jnp.right_shift(col, lh + 1)
        lower_left = (jnp.bitwise_and(jnp.right_shift(row, lh), 1) == 1) & \
                     (jnp.bitwise_and(jnp.right_shift(col, lh), 1) == 0)
        levels.append(jnp.where(blk & lower_left, 1.0, 0.0))
    masks = (m0, m1, strict, incl, eye, levels)

    where = [(bi, slice(i * lanes, (i + 1) * lanes))
             for bi in range(r_ref.shape[0]) for i in range(r_ref.shape[2] // lanes)]
    streams = [tuple(ref[bi, :, cs].astype(F32) for ref in (r_ref, ld_ref, k_ref, v_ref, kk_ref, b_ref))
               for bi, cs in where]
    ys, ht_new = _rwkv_chunk(streams, [ht_ref[i] for i in range(len(where))], masks)
    for i in range(len(where)):
        ht_ref[i] = ht_new[i]
    inv_n = 1.0 / B_HEAD_DIM
    means = [jnp.dot(y.astype(BF16), ones_ref[...], preferred_element_type=F32) * inv_n for y in ys]
    ds = [y - mean for y, mean in zip(ys, means)]
    vrs = [jnp.dot((d * d).astype(BF16), ones_ref[...], preferred_element_type=F32) * inv_n for d in ds]
    for (bi, cs), d, var in zip(where, ds, vrs):
        yn = d * lax.rsqrt(var + GN_EPS) * gw_ref[:, cs] + gb_ref[:, cs]
        o_ref[bi, :, cs] = ((yn + bonus_ref[bi, :, cs].astype(F32)) * g_ref[bi, :, cs].astype(F32)
                            ).astype(o_ref.dtype)


def rwkv_scan(r, ld, k, v, kk, b, bonus, g, gn_w, gn_b, name):
    bsz, s, bw = r.shape
    c = RWKV_CHUNK
    nb = _pick(bsz, (RWKV_SEQS_PER_STEP, 1))
    assert s % c == 0 and c & (c - 1) == 0 and 2 * B_HEAD_DIM == V7X_LANES
    tok = pl.BlockSpec((nb, c, bw), lambda bi, i: (bi, i, 0))
    par = pl.BlockSpec((1, bw), lambda bi, i: (0, 0))
    chains = nb * (bw // V7X_LANES)
    est = (18 * nb * _nbytes((c, bw), F32) + chains * _nbytes((V7X_LANES, V7X_LANES), F32)
           + 8 * chains * _nbytes((4 * c, 4 * c), F32))
    return pl.pallas_call(
        _rwkv_scan_body,
        grid=(bsz // nb, s // c),
        in_specs=[tok] * 8 + [par] * 2 + [pl.BlockSpec((V7X_LANES, V7X_LANES), lambda bi, i: (0, 0))],
        out_specs=tok,
        out_shape=jax.ShapeDtypeStruct((bsz, s, bw), BF16),
        scratch_shapes=[pltpu.VMEM((chains, V7X_LANES, V7X_LANES), F32)],
        compiler_params=pltpu.CompilerParams(
            dimension_semantics=("parallel", "arbitrary"), vmem_limit_bytes=_vmem_limit(est)),
        name=name,
    )(r, ld, k, v, kk, b, bonus, g, gn_w.reshape(1, bw), gn_b.reshape(1, bw), _head_pair_ones())


def _lora_proj_weight(w_in, has_vres):
    o = 3 * A_WIDTH + 3 * B_WIDTH
    d = w_in.shape[0]
    wz = jnp.zeros((d, Z_WIDTH), w_in.dtype)
    wz = wz.at[:, ZW_OFF:ZW_OFF + LORA_W].set(w_in[:, o:o + LORA_W])
    o += LORA_W
    wz = wz.at[:, ZA_OFF:ZA_OFF + LORA_A].set(w_in[:, o:o + LORA_A])
    o += LORA_A
    wz = wz.at[:, ZG_OFF:ZG_OFF + LORA_G].set(w_in[:, o:o + LORA_G])
    o += LORA_G
    if has_vres:
        wz = wz.at[:, ZV_OFF:ZV_OFF + LORA_V].set(w_in[:, o:o + LORA_V])
    return wz


def moba_rwkv_layer(x, b, s, li, norm_mix, w_in, q_gain, k_gain, shift_mu, w0, w_lora, a0, a_lora, g_lora,
                    k_k, k_a, r_k, gn_w, gn_b, w_out, v0, v_lora, v_first):
    has_vres = v_lora is not None
    tag = "l%d_" % li
    n_main = 3 * A_WIDTH + 3 * B_WIDTH
    proj_a, proj_b = norm_matmul_split(x, norm_mix, w_in[:, :n_main].astype(BF16), 3 * A_WIDTH, F32, BF16,
                                       tag + "in_proj")
    pz = norm_matmul(x, norm_mix, _lora_proj_weight(w_in, has_vres).astype(BF16), F32, tag + "lora_proj")
    pz = pz.reshape(b, s, Z_WIDTH)

    y_a = moba_attention(proj_a.reshape(b, s, 3 * A_WIDTH), q_gain, k_gain, tag + "moba")

    r, ld, k, v, kk, kb, bonus, g = rwkv_prep(proj_b.reshape(b, s, 3 * B_WIDTH), pz, shift_mu, w0, w_lora, a0,
                                              a_lora, g_lora, k_k, k_a, r_k.reshape(-1), v0, v_lora, v_first,
                                              tag + "rwkv_prep")
    if not has_vres:
        v_first = v
    y_b = rwkv_scan(r, ld, k, v, kk, kb, bonus, g, gn_w, gn_b, tag + "rwkv_scan")

    x = mix_out_proj(y_a.reshape(b * s, A_WIDTH), y_b.reshape(b * s, B_WIDTH), w_out.astype(BF16), x,
                     tag + "out_proj")
    return x, v_first


def short_conv_layer(x, b, s, li, norm_mix, conv_in, conv_w, conv_out):
    tag = "l%d_" % li
    m = norm_matmul_conv(x, norm_mix, conv_in.astype(BF16), conv_w, s, 3, _short_conv_combine, tag + "conv_in")
    return matmul_residual(m, conv_out.astype(BF16), x, tag + "conv_out")


def conv_ffn(x, b, s, li, norm_ffn, ffn_up, ffn_conv, ffn_down):
    tag = "l%d_" % li
    m = norm_matmul_conv(x, norm_ffn, ffn_up.astype(BF16), ffn_conv, s, 2, _ffn_combine, tag + "ffn_up")
    return matmul_residual(m, ffn_down.astype(BF16), x, tag + "ffn_down")


def kernel(x, l0_norm_mix, l0_w_in, l0_q_gain, l0_k_gain, l0_shift_mu, l0_w0, l0_w_lora, l0_a0, l0_a_lora, l0_g_lora, l0_k_k, l0_k_a, l0_r_k, l0_gn_w, l0_gn_b, l0_w_out, l0_norm_ffn, l0_ffn_up, l0_ffn_conv, l0_ffn_down, l1_norm_mix, l1_conv_in, l1_conv_w, l1_conv_out, l1_norm_ffn, l1_ffn_up, l1_ffn_conv, l1_ffn_down, l2_norm_mix, l2_w_in, l2_q_gain, l2_k_gain, l2_shift_mu, l2_w0, l2_w_lora, l2_a0, l2_a_lora, l2_g_lora, l2_k_k, l2_k_a, l2_r_k, l2_gn_w, l2_gn_b, l2_w_out, l2_v0, l2_v_lora, l2_norm_ffn, l2_ffn_up, l2_ffn_conv, l2_ffn_down, l3_norm_mix, l3_conv_in, l3_conv_w, l3_conv_out, l3_norm_ffn, l3_ffn_up, l3_ffn_conv, l3_ffn_down):
    b, s, d = x.shape
    h = x.reshape(b * s, d)

    h, v_first = moba_rwkv_layer(
        h, b, s, 0, l0_norm_mix, l0_w_in, l0_q_gain, l0_k_gain, l0_shift_mu, l0_w0, l0_w_lora, l0_a0,
        l0_a_lora, l0_g_lora, l0_k_k, l0_k_a, l0_r_k, l0_gn_w, l0_gn_b, l0_w_out, None, None, None)
    h = conv_ffn(h, b, s, 0, l0_norm_ffn, l0_ffn_up, l0_ffn_conv, l0_ffn_down)

    h = short_conv_layer(h, b, s, 1, l1_norm_mix, l1_conv_in, l1_conv_w, l1_conv_out)
    h = conv_ffn(h, b, s, 1, l1_norm_ffn, l1_ffn_up, l1_ffn_conv, l1_ffn_down)

    h, _ = moba_rwkv_layer(
        h, b, s, 2, l2_norm_mix, l2_w_in, l2_q_gain, l2_k_gain, l2_shift_mu, l2_w0, l2_w_lora, l2_a0,
        l2_a_lora, l2_g_lora, l2_k_k, l2_k_a, l2_r_k, l2_gn_w, l2_gn_b, l2_w_out, l2_v0, l2_v_lora, v_first)
    h = conv_ffn(h, b, s, 2, l2_norm_ffn, l2_ffn_up, l2_ffn_conv, l2_ffn_down)

    h = short_conv_layer(h, b, s, 3, l3_norm_mix, l3_conv_in, l3_conv_w, l3_conv_out)
    h = conv_ffn(h, b, s, 3, l3_norm_ffn, l3_ffn_up, l3_ffn_conv, l3_ffn_down)
    return h.reshape(b, s, d)
```

```python
import functools

import jax
import jax.numpy as jnp
from jax import lax
from jax.experimental import pallas as pl
from jax.experimental.pallas import tpu as pltpu

F32 = jnp.float32
BF16 = jnp.bfloat16

A_HEADS = 8
A_HEAD_DIM = 128
A_WIDTH = A_HEADS * A_HEAD_DIM
MOBA_BLOCK = 256
MOBA_TOPK = 3
MOBA_HEADS_PER_STEP = 4
MOBA_DENOM_ROWS = 16
B_HEADS = 16
B_HEAD_DIM = 64
B_WIDTH = B_HEADS * B_HEAD_DIM
RWKV_CHUNK = 64
RWKV_SEQS_PER_STEP = 2
LORA_W = 64
LORA_A = 64
LORA_G = 160
LORA_V = 32
CONV_WIDTH = 3
RMS_EPS = 1e-6
GN_EPS = 64e-5
NEG_INF = -1e30
LOG2E = 1.4426950408889634

V7X_LANES = 128
V7X_SUBLANES = 8
V7X_VMEM_BYTES = 64 * 1024 * 1024
V7X_VMEM_CAP = V7X_VMEM_BYTES - 8 * 1024 * 1024
SPILL_ALLOWANCE = 4 * 1024 * 1024
CONV_ROW_CHUNK = 256

ZW_OFF, ZW_PAD = 0, 128
ZA_OFF, ZA_PAD = 128, 128
ZG_OFF, ZG_PAD = 256, 256
ZV_OFF, ZV_PAD = 512, 128
Z_WIDTH = 640


def _vmem_limit(nbytes):
    return int(min(max(nbytes + SPILL_ALLOWANCE, 16 * 1024 * 1024), V7X_VMEM_CAP))


def _nbytes(shape, dtype):
    n = 1
    for s in shape:
        n *= s
    return n * jnp.dtype(dtype).itemsize


def _pick(n, prefs):
    for p in prefs:
        if n % p == 0:
            return p
    return n


def _norm_matmul_split_body(x_ref, g_ref, w_ref, wz_ref, oa_ref, ob_ref, oz_ref, xn_ref, *, na, nt):
    j = pl.program_id(1)

    @pl.when(j == 0)
    def _():
        x = x_ref[...]
        ms = jnp.mean(x * x, axis=-1, keepdims=True)
        xn_ref[...] = (x * lax.rsqrt(ms + RMS_EPS) * g_ref[...]).astype(BF16)

    @pl.when(j < nt)
    def _():
        h = jnp.dot(xn_ref[...], w_ref[...], preferred_element_type=F32)

        @pl.when(j < na)
        def _():
            oa_ref[...] = h.astype(oa_ref.dtype)

        @pl.when(j >= na)
        def _():
            ob_ref[...] = h.astype(ob_ref.dtype)

    @pl.when(j == nt)
    def _():
        oz_ref[...] = jnp.dot(xn_ref[...], wz_ref[...], preferred_element_type=F32).astype(oz_ref.dtype)


def norm_matmul_split(x, gain, w, n, n_a, wz, dtype_a, dtype_b, dtype_z, name):
    m, k = x.shape
    nz = wz.shape[1]
    tm = _pick(m, (1024, 512, 256, 128))
    tn = _pick(n_a, (1024, 512, 256, 128))
    assert n % tn == 0
    na = n_a // tn
    nt = n // tn
    est = (2 * _nbytes((tm, k), F32) + _nbytes((tm, k), BF16) + 2 * _nbytes((k, tn + nz), BF16)
           + 8 * _nbytes((tm, tn), F32) + 3 * _nbytes((tm, nz), F32) + _nbytes((tm, k), F32))
    return pl.pallas_call(
        functools.partial(_norm_matmul_split_body, na=na, nt=nt),
        grid=(m // tm, nt + 1),
        in_specs=[
            pl.BlockSpec((tm, k), lambda i, j: (i, 0)),
            pl.BlockSpec((1, k), lambda i, j: (0, 0)),
            pl.BlockSpec((k, tn), lambda i, j: (0, jnp.minimum(j, nt - 1))),
            pl.BlockSpec((k, nz), lambda i, j: (0, 0)),
        ],
        out_specs=[
            pl.BlockSpec((tm, tn), lambda i, j: (i, jnp.minimum(j, na - 1))),
            pl.BlockSpec((tm, tn), lambda i, j: (i, jnp.clip(j - na, 0, nt - na - 1))),
            pl.BlockSpec((tm, nz), lambda i, j: (i, 0)),
        ],
        out_shape=[jax.ShapeDtypeStruct((m, n_a), dtype_a), jax.ShapeDtypeStruct((m, n - n_a), dtype_b),
                   jax.ShapeDtypeStruct((m, nz), dtype_z)],
        scratch_shapes=[pltpu.VMEM((tm, k), BF16)],
        compiler_params=pltpu.CompilerParams(
            dimension_semantics=("arbitrary", "arbitrary"), vmem_limit_bytes=_vmem_limit(est)),
        name=name,
    )(x, gain.reshape(1, k), w, wz)


def _conv3_ext(u, w):
    return w[0:1, :] * pltpu.roll(u, 2, 0) + w[1:2, :] * pltpu.roll(u, 1, 0) + w[2:3, :] * u


def _norm_matmul_conv_body(x_ref, g_ref, *refs, n_groups, n_taps, tiles_per_seq, combine):
    w_refs = refs[:n_groups]
    cw_refs = refs[n_groups:n_groups + n_taps]
    o_ref, xn_ref, carry_ref = refs[n_groups + n_taps:]
    i = pl.program_id(0)
    j = pl.program_id(1)
    tm, tn = o_ref.shape
    sub = V7X_SUBLANES
    rc = min(tm, CONV_ROW_CHUNK)

    @pl.when(j == 0)
    def _():
        x = x_ref[...]
        ms = jnp.mean(x * x, axis=-1, keepdims=True)
        xn_ref[...] = (x * lax.rsqrt(ms + RMS_EPS) * g_ref[...]).astype(BF16)

    @pl.when(i % tiles_per_seq == 0)
    def _():
        carry_ref[j] = jnp.zeros(carry_ref.shape[1:], F32)

    w = jnp.concatenate([w_ref[...] for w_ref in w_refs], axis=1)
    h = jnp.dot(xn_ref[...], w, preferred_element_type=F32)
    taps = [cw_ref[...] for cw_ref in cw_refs]
    for c in range(tm // rc):
        r0 = c * rc
        if c == 0:
            ext = jnp.concatenate([carry_ref[j], h[:rc, :]], axis=0)
        else:
            ext = h[r0 - sub:r0 + rc, :]
        groups = [ext[:, g * tn:(g + 1) * tn] for g in range(n_groups)]
        o_ref[r0:r0 + rc, :] = combine(groups, taps)[sub:, :].astype(o_ref.dtype)
    carry_ref[j] = h[tm - sub:, :]


def _ffn_combine(groups, taps):
    return jax.nn.silu(_conv3_ext(groups[0], taps[0])) * _conv3_ext(groups[1], taps[1])


def _short_conv_combine(groups, taps):
    return groups[0] * _conv3_ext(groups[1] * groups[2], taps[0])


def norm_matmul_conv(x, gain, w, conv_w, seq, n_groups, combine, name):
    m, k = x.shape
    n = w.shape[1] // n_groups
    tm = _pick(seq, (1024, 512, 256, 128))
    tn = _pick(n, (512, 256, 128))
    nj = n // tn
    n_cw = conv_w.shape[1] // n
    est = (2 * _nbytes((tm, k), F32) + _nbytes((tm, k), BF16) + 3 * n_groups * _nbytes((k, tn), BF16)
           + 2 * _nbytes((tm, tn), BF16) + (n_groups + 2) * _nbytes((tm, tn), F32)
           + 8 * n_groups * _nbytes((CONV_ROW_CHUNK, tn), F32) + nj * n_groups * _nbytes((V7X_SUBLANES, tn), F32))
    w_specs = [pl.BlockSpec((k, tn), lambda i, j, g=g: (0, g * nj + j)) for g in range(n_groups)]
    cw_specs = [pl.BlockSpec((CONV_WIDTH, tn), lambda i, j, g=g: (0, g * nj + j)) for g in range(n_cw)]
    return pl.pallas_call(
        functools.partial(_norm_matmul_conv_body, n_groups=n_groups, n_taps=n_cw, tiles_per_seq=seq // tm,
                          combine=combine),
        grid=(m // tm, nj),
        in_specs=[pl.BlockSpec((tm, k), lambda i, j: (i, 0)), pl.BlockSpec((1, k), lambda i, j: (0, 0))]
        + w_specs + cw_specs,
        out_specs=pl.BlockSpec((tm, tn), lambda i, j: (i, j)),
        out_shape=jax.ShapeDtypeStruct((m, n), BF16),
        scratch_shapes=[pltpu.VMEM((tm, k), BF16),
                        pltpu.VMEM((nj, V7X_SUBLANES, n_groups * tn), F32)],
        compiler_params=pltpu.CompilerParams(
            dimension_semantics=("arbitrary", "arbitrary"), vmem_limit_bytes=_vmem_limit(est)),
        name=name,
    )(x, gain.reshape(1, k), *([w] * n_groups), *([conv_w] * n_cw))


def _matmul_res_body(a_ref, w_ref, r_ref, o_ref):
    o_ref[...] = r_ref[...] + jnp.dot(a_ref[...], w_ref[...], preferred_element_type=F32)


def matmul_residual(a, w, res, name):
    m, k = a.shape
    n = w.shape[1]
    tm = _pick(m, (1024, 512, 256, 128))
    tn = _pick(n, (512, 256, 128)) if k > 4096 else _pick(n, (1024, 512, 256, 128))
    est = (2 * _nbytes((tm, k), BF16) + 2 * _nbytes((k, tn), BF16) + 6 * _nbytes((tm, tn), F32))
    return pl.pallas_call(
        _matmul_res_body,
        grid=(m // tm, n // tn),
        in_specs=[
            pl.BlockSpec((tm, k), lambda i, j: (i, 0)),
            pl.BlockSpec((k, tn), lambda i, j: (0, j)),
            pl.BlockSpec((tm, tn), lambda i, j: (i, j)),
        ],
        out_specs=pl.BlockSpec((tm, tn), lambda i, j: (i, j)),
        out_shape=jax.ShapeDtypeStruct((m, n), F32),
        compiler_params=pltpu.CompilerParams(
            dimension_semantics=("parallel", "parallel"), vmem_limit_bytes=_vmem_limit(est)),
        name=name,
    )(a, w, res)


def _mix_out_body(ylo_ref, yhi_ref, yb_ref, wa_ref, wb_ref, r_ref, o_ref, *, tiles_per_half):
    base = r_ref[...] + jnp.dot(yb_ref[...], wb_ref[...], preferred_element_type=F32)
    q = pl.program_id(0) % (2 * tiles_per_half)

    @pl.when(q < tiles_per_half)
    def _():
        o_ref[...] = base + jnp.dot(ylo_ref[...], wa_ref[...], preferred_element_type=F32)

    @pl.when(q >= tiles_per_half)
    def _():
        o_ref[...] = base + jnp.dot(yhi_ref[...], wa_ref[...], preferred_element_type=F32)


def mix_out_proj(ya_lo, ya_hi, yb, w_out, res, seq, name):
    m, kb = yb.shape
    ka = ya_lo.shape[1]
    n = w_out.shape[1]
    half = seq // 2
    tm = _pick(half, (1024, 512, 256, 128))
    tn = _pick(n, (1024, 512, 256, 128))
    tph = half // tm

    def half_tile(i, first):
        q = i % (2 * tph)
        within = jnp.minimum(q, tph - 1) if first else jnp.maximum(q - tph, 0)
        return (i // (2 * tph)) * tph + within

    est = 2 * _nbytes((tm, 2 * ka + kb), BF16) + 2 * _nbytes((ka + kb, tn), BF16) + 8 * _nbytes((tm, tn), F32)
    return pl.pallas_call(
        functools.partial(_mix_out_body, tiles_per_half=tph),
        grid=(m // tm, n // tn),
        in_specs=[
            pl.BlockSpec((tm, ka), lambda i, j: (half_tile(i, True), 0)),
            pl.BlockSpec((tm, ka), lambda i, j: (half_tile(i, False), 0)),
            pl.BlockSpec((tm, kb), lambda i, j: (i, 0)),
            pl.BlockSpec((ka, tn), lambda i, j: (0, j)),
            pl.BlockSpec((kb, tn), lambda i, j: (0, j)),
            pl.BlockSpec((tm, tn), lambda i, j: (i, j)),
        ],
        out_specs=pl.BlockSpec((tm, tn), lambda i, j: (i, j)),
        out_shape=jax.ShapeDtypeStruct((m, n), F32),
        compiler_params=pltpu.CompilerParams(
            dimension_semantics=("parallel", "parallel"), vmem_limit_bytes=_vmem_limit(est)),
        name=name,
    )(ya_lo, ya_hi, yb, w_out[:ka], w_out[ka:], res)


def _head_rms(x, gain):
    return x * lax.rsqrt(jnp.mean(x * x, axis=-1, keepdims=True) + RMS_EPS) * gain


def _moba_query_block(ii, q, q_gain, slope, rel_t, kn_ref, vt_ref, km_ref):
    bs = MOBA_BLOCK
    hd = A_HEAD_DIM
    qn = _head_rms(q, q_gain)
    qs = (qn * (hd ** -0.5 * LOG2E)).astype(BF16)
    slope2 = slope * LOG2E
    bias_rel = slope2 * rel_t
    gt = lax.dot_general(km_ref[...], qn, (((1,), (1,)), ((), ())),
                         precision=lax.Precision.HIGHEST, preferred_element_type=F32)
    blk = lax.broadcasted_iota(jnp.int32, gt.shape, 0)
    gm = jnp.where(blk < ii, gt, NEG_INF)
    scores = []
    m_all = None
    for n in range(ii + 1):
        s = lax.dot_general(kn_ref[n * bs:(n + 1) * bs, :], qs, (((1,), (1,)), ((), ())),
                            preferred_element_type=F32) - bias_rel
        if n == ii:
            s = jnp.where(rel_t >= 0, s, NEG_INF)
            shift = jnp.zeros((1, bs), F32)
        else:
            g_n = gm[n:n + 1, :]
            beats = jnp.where((gm > g_n) | ((gm == g_n) & (blk < n)), 1.0, 0.0)
            chosen = jnp.sum(beats, axis=0, keepdims=True) < float(MOBA_TOPK)
            shift = jnp.where(chosen, slope2 * float((ii - n) * bs), -NEG_INF)
        m_blk = jnp.max(s, axis=0, keepdims=True) - shift
        m_all = m_blk if m_all is None else jnp.maximum(m_all, m_blk)
        scores.append((s, shift))
    acc = None
    for n, (s, shift) in enumerate(scores):
        p = jnp.exp2(s - (m_all + shift)).astype(BF16)
        pv = jnp.dot(vt_ref[:, n * bs:(n + 1) * bs], p, preferred_element_type=F32)
        acc = pv if acc is None else acc + pv
    return (acc[:hd] / acc[hd:hd + 1]).T


def _moba_body(qlo_ref, qhi_ref, k_ref, v_ref, qg_ref, kg_ref, olo_ref, ohi_ref, kn_ref, vt_ref, km_ref,
               *, n_heads, n_blocks):
    bs = MOBA_BLOCK
    hd = A_HEAD_DIM
    hp = pl.program_id(1)
    j = pl.program_id(2)

    @pl.when(j == 0)
    def _():
        km_ref[...] = jnp.zeros_like(km_ref)
        for e in range(MOBA_HEADS_PER_STEP):
            cols = slice(e * hd, (e + 1) * hd)
            for nb in range(n_blocks):
                rows = slice(nb * bs, (nb + 1) * bs)
                kn = _head_rms(k_ref[0, rows, cols], kg_ref[...])
                kn_ref[e, rows, :] = kn.astype(BF16)
                km_ref[e, nb:nb + 1, :] = jnp.mean(kn, axis=0, keepdims=True)
                vt_ref[e, :hd, rows] = v_ref[0, rows, cols].T.astype(BF16)
            vt_ref[e, hd:, :] = jnp.ones((MOBA_DENOM_ROWS, vt_ref.shape[2]), BF16)

    rel_t = (lax.broadcasted_iota(jnp.int32, (bs, bs), 1)
             - lax.broadcasted_iota(jnp.int32, (bs, bs), 0)).astype(F32)

    for jj in range(n_blocks // 2):
        @pl.when(j == jj)
        def _(jj=jj):
            for e in range(MOBA_HEADS_PER_STEP):
                cols = slice(e * hd, (e + 1) * hd)
                head = (hp * MOBA_HEADS_PER_STEP + e + 1).astype(F32)
                slope = jnp.exp2(jnp.full((1, bs), -8.0 / n_heads, F32) * head)
                for q_ref, o_ref, ii in ((qlo_ref, olo_ref, jj), (qhi_ref, ohi_ref, n_blocks - 1 - jj)):
                    o = _moba_query_block(ii, q_ref[0, :, cols], qg_ref[...], slope, rel_t,
                                          kn_ref.at[e], vt_ref.at[e], km_ref.at[e])
                    o_ref[0, :, cols] = o.astype(o_ref.dtype)


def moba_attention(proj, q_gain, k_gain, name):
    b, s, _ = proj.shape
    assert s % MOBA_BLOCK == 0
    nb = s // MOBA_BLOCK
    assert nb % 2 == 0 and A_HEADS % MOBA_HEADS_PER_STEP == 0
    nbp = -(-nb // V7X_SUBLANES) * V7X_SUBLANES
    hd = A_HEAD_DIM
    hw = MOBA_HEADS_PER_STEP * hd
    groups = A_HEADS // MOBA_HEADS_PER_STEP
    half = nb // 2
    est = (8 * _nbytes((MOBA_BLOCK, hw), F32) + 4 * _nbytes((s, hw), F32) + 2 * _nbytes((s, hw), BF16)
           + MOBA_HEADS_PER_STEP * (4 * nb + 16) * _nbytes((MOBA_BLOCK, MOBA_BLOCK), F32))
    o_lo, o_hi = pl.pallas_call(
        functools.partial(_moba_body, n_heads=A_HEADS, n_blocks=nb),
        grid=(b, groups, half),
        in_specs=[
            pl.BlockSpec((1, MOBA_BLOCK, hw), lambda bi, g, j: (bi, j, g)),
            pl.BlockSpec((1, MOBA_BLOCK, hw), lambda bi, g, j: (bi, nb - 1 - j, g)),
            pl.BlockSpec((1, s, hw), lambda bi, g, j: (bi, 0, groups + g)),
            pl.BlockSpec((1, s, hw), lambda bi, g, j: (bi, 0, 2 * groups + g)),
            pl.BlockSpec((1, hd), lambda bi, g, j: (0, 0)),
            pl.BlockSpec((1, hd), lambda bi, g, j: (0, 0)),
        ],
        out_specs=[
            pl.BlockSpec((1, MOBA_BLOCK, hw), lambda bi, g, j: (bi, j, g)),
            pl.BlockSpec((1, MOBA_BLOCK, hw), lambda bi, g, j: (bi, half - 1 - j, g)),
        ],
        out_shape=[jax.ShapeDtypeStruct((b, s // 2, A_WIDTH), BF16)] * 2,
        scratch_shapes=[
            pltpu.VMEM((MOBA_HEADS_PER_STEP, s, hd), BF16),
            pltpu.VMEM((MOBA_HEADS_PER_STEP, hd + MOBA_DENOM_ROWS, s), BF16),
            pltpu.VMEM((MOBA_HEADS_PER_STEP, nbp, hd), F32),
        ],
        compiler_params=pltpu.CompilerParams(
            dimension_semantics=("parallel", "parallel", "arbitrary"), vmem_limit_bytes=_vmem_limit(est)),
        name=name,
    )(proj, proj, proj, proj, q_gain.reshape(1, hd), k_gain.reshape(1, hd))
    return o_lo, o_hi


def _token_shift(p, prev_last, mu):
    rolled = pltpu.roll(p, 1, 0)
    row = lax.broadcasted_iota(jnp.int32, p.shape, 0)
    p_prev = jnp.where(row == 0, prev_last, rolled)
    return p + mu * (p_prev - p)


def _head_sums(x, ones_ref):
    tiles = [jnp.dot(x[:, c:c + V7X_LANES].astype(BF16), ones_ref[...], preferred_element_type=F32)
             for c in range(0, x.shape[1], V7X_LANES)]
    return jnp.concatenate(tiles, axis=1)


def _rwkv_prep_body(*refs, has_vres):
    if has_vres:
        (pm_ref, pz_ref, mum_ref, muz_ref, w0_ref, wl_ref, a0_ref, al_ref, gl_ref, kk_ref, ka_ref, rk_ref,
         ones_ref, v0_ref, vl_ref, vf_ref,
         r_ref, ld_ref, k_ref, v_ref, kn_ref, b_ref, bonus_ref, g_ref, cm_ref, cz_ref) = refs
    else:
        (pm_ref, pz_ref, mum_ref, muz_ref, w0_ref, wl_ref, a0_ref, al_ref, gl_ref, kk_ref, ka_ref, rk_ref,
         ones_ref,
         r_ref, ld_ref, k_ref, v_ref, kn_ref, b_ref, bonus_ref, g_ref, cm_ref, cz_ref) = refs
    tm = pm_ref.shape[1]
    bw = B_WIDTH

    @pl.when(pl.program_id(1) == 0)
    def _():
        cm_ref[...] = jnp.zeros_like(cm_ref)
        cz_ref[...] = jnp.zeros_like(cz_ref)

    pm = pm_ref[0].astype(F32)
    pz = pz_ref[0]
    last = V7X_SUBLANES - 1
    pms = _token_shift(pm, cm_ref[last:last + 1, :], mum_ref[...])
    pzs = _token_shift(pz, cz_ref[last:last + 1, :], muz_ref[...])
    cm_ref[...] = pm[tm - V7X_SUBLANES:, :]
    cz_ref[...] = pz[tm - V7X_SUBLANES:, :]

    r = pms[:, :bw]
    k = pms[:, bw:2 * bw]
    v = pms[:, 2 * bw:3 * bw]

    z_w = jnp.tanh(pzs[:, ZW_OFF:ZW_OFF + ZW_PAD]).astype(BF16)
    w = w0_ref[...] + jnp.dot(z_w, wl_ref[...], preferred_element_type=F32)
    w = -jax.nn.softplus(-w) - 0.5
    ld_ref[0] = -jnp.exp(w)

    z_a = pzs[:, ZA_OFF:ZA_OFF + ZA_PAD].astype(BF16)
    a = jax.nn.sigmoid(a0_ref[...] + jnp.dot(z_a, al_ref[...], preferred_element_type=F32))

    z_g = jax.nn.sigmoid(pzs[:, ZG_OFF:ZG_OFF + ZG_PAD]).astype(BF16)
    g_ref[0] = jnp.dot(z_g, gl_ref[...], preferred_element_type=F32).astype(g_ref.dtype)

    if has_vres:
        z_v = pzs[:, ZV_OFF:ZV_OFF + ZV_PAD].astype(BF16)
        mix = jax.nn.sigmoid(v0_ref[...] + jnp.dot(z_v, vl_ref[...], preferred_element_type=F32))
        v = v + (vf_ref[0].astype(F32) - v) * mix

    kk = k * kk_ref[...]
    kk = kk * jnp.minimum(lax.rsqrt(_head_sums(kk * kk, ones_ref)), 1e12)
    k = k * (1.0 + (a - 1.0) * ka_ref[...])
    r_ref[0] = r.astype(r_ref.dtype)
    k_ref[0] = k.astype(k_ref.dtype)
    v_ref[0] = v.astype(v_ref.dtype)
    kn_ref[0] = kk.astype(kn_ref.dtype)
    b_ref[0] = (kk * a).astype(b_ref.dtype)
    bonus_ref[0] = (_head_sums(r * k * rk_ref[...], ones_ref) * v).astype(bonus_ref.dtype)


def _pad_rows(w, rows):
    return jnp.pad(w, ((0, rows - w.shape[0]), (0, 0)))


def _head_pair_ones():
    head = jnp.arange(V7X_LANES) // B_HEAD_DIM
    return (head[:, None] == head[None, :]).astype(BF16)


def rwkv_prep(proj, pz, shift_mu, w0, w_lora, a0, a_lora, g_lora, k_k, k_a, r_k, v0, v_lora, v_first, name):
    b, s, main_blk = proj.shape
    bw = B_WIDTH
    has_vres = v_lora is not None
    tm = _pick(s, (256, 128))
    assert main_blk == 3 * bw

    mu_main = shift_mu[:3 * bw].reshape(1, 3 * bw)
    lora_mu = shift_mu[3 * bw:]
    mu_z = jnp.zeros((Z_WIDTH,), F32)
    mu_z = mu_z.at[ZW_OFF:ZW_OFF + LORA_W].set(lora_mu[:LORA_W])
    mu_z = mu_z.at[ZA_OFF:ZA_OFF + LORA_A].set(lora_mu[LORA_W:LORA_W + LORA_A])
    mu_z = mu_z.at[ZG_OFF:ZG_OFF + LORA_G].set(lora_mu[LORA_W + LORA_A:LORA_W + LORA_A + LORA_G])
    if has_vres:
        mu_z = mu_z.at[ZV_OFF:ZV_OFF + LORA_V].set(lora_mu[LORA_W + LORA_A + LORA_G:])
    mu_z = mu_z.reshape(1, Z_WIDTH)

    row = lambda t: t.reshape(1, bw)
    const = lambda shape: pl.BlockSpec(shape, lambda bi, i: (0, 0))
    tok = pl.BlockSpec((1, tm, bw), lambda bi, i: (bi, i, 0))
    args = [proj, pz, mu_main, mu_z, row(w0), _pad_rows(w_lora, ZW_PAD).astype(BF16),
            row(a0), _pad_rows(a_lora, ZA_PAD).astype(BF16), _pad_rows(g_lora, ZG_PAD).astype(BF16),
            row(k_k), row(k_a), row(r_k), _head_pair_ones()]
    in_specs = [
        pl.BlockSpec((1, tm, main_blk), lambda bi, i: (bi, i, 0)),
        pl.BlockSpec((1, tm, Z_WIDTH), lambda bi, i: (bi, i, 0)),
        const((1, main_blk)), const((1, Z_WIDTH)),
        const((1, bw)), const((ZW_PAD, bw)), const((1, bw)), const((ZA_PAD, bw)), const((ZG_PAD, bw)),
        const((1, bw)), const((1, bw)), const((1, bw)), const((V7X_LANES, V7X_LANES)),
    ]
    if has_vres:
        args += [row(v0), _pad_rows(v_lora, ZV_PAD).astype(BF16), v_first]
        in_specs += [const((1, bw)), const((ZV_PAD, bw)), tok]
    out_dtypes = [BF16, F32, BF16, BF16, BF16, BF16, BF16, BF16]
    n_out = len(out_dtypes)
    est = (2 * _nbytes((tm, main_blk + Z_WIDTH), F32) + (2 * n_out + 6) * _nbytes((tm, bw), F32)
           + 6 * _nbytes((tm, main_blk), F32) + 4 * _nbytes((ZG_PAD + 3 * ZW_PAD, bw), BF16))
    return pl.pallas_call(
        functools.partial(_rwkv_prep_body, has_vres=has_vres),
        grid=(b, s // tm),
        in_specs=in_specs,
        out_specs=[tok] * n_out,
        out_shape=[jax.ShapeDtypeStruct((b, s, bw), dt) for dt in out_dtypes],
        scratch_shapes=[pltpu.VMEM((V7X_SUBLANES, main_blk), F32), pltpu.VMEM((V7X_SUBLANES, Z_WIDTH), F32)],
        compiler_params=pltpu.CompilerParams(
            dimension_semantics=("parallel", "arbitrary"), vmem_limit_bytes=_vmem_limit(est)),
        name=name,
    )(*args)


def _cumsum_rows(x):
    row = lax.broadcasted_iota(jnp.int32, x.shape, 0)
    sh = 1
    while sh < x.shape[0]:
        x = x + jnp.where(row >= sh, pltpu.roll(x, sh, 0), 0.0)
        sh *= 2
    return x


def _stack_heads(x, m0, m1):
    return jnp.concatenate([x * m0, x * m1], axis=0)


def _nt_dot(a, b):
    return lax.dot_general(a.astype(BF16), b.astype(BF16), (((1,), (1,)), ((), ())), preferred_element_type=F32)


def _tn_dot(a, b):
    return lax.dot_general(a.astype(BF16), b.astype(BF16), (((0,), (0,)), ((), ())), preferred_element_type=F32)


def _nn_dot(a, b):
    return jnp.dot(a.astype(BF16), b.astype(BF16), preferred_element_type=F32)


def _rwkv_chunk(streams, hts, masks):
    pairs = range(len(streams))
    c = streams[0][0].shape[0]
    n2 = 2 * c
    m0, m1, strict, incl, eye, levels = masks
    ar, bk, v_st, p_last = [], [], [], []
    for r, ld, k, v, kk, b in streams:
        cs = _cumsum_rows(ld)
        p = jnp.exp(cs)
        p_inv = jnp.exp(-cs)
        a_st = _stack_heads(-kk * jnp.exp(cs - ld), m0, m1)
        r_st = _stack_heads(r * p, m0, m1)
        b_st = _stack_heads(b * p_inv, m0, m1)
        k_st = _stack_heads(k * p_inv, m0, m1)
        ar.append(jnp.concatenate([a_st, r_st], axis=0).astype(BF16))
        bk.append(jnp.concatenate([b_st, k_st], axis=0).astype(BF16))
        v_st.append(_stack_heads(v, m0, m1))
        p_last.append(p[c - 1:c, :])

    q = [_nt_dot(ar[i], bk[i]) for i in pairs]
    l_ab = [q[i][:n2, :n2] * strict for i in pairs]
    t = [eye + l_ab[i] * levels[0] for i in pairs]
    for lvl in levels[1:]:
        tc = [_nn_dot(t[i], l_ab[i] * lvl) for i in pairs]
        t = [t[i] + _nn_dot(tc[i], t[i]) for i in pairs]

    ar_h = [_nt_dot(ar[i], hts[i]) for i in pairs]
    ak_v = [_nn_dot(q[i][:n2, n2:] * strict, v_st[i]) for i in pairs]
    u_st = [_nn_dot(t[i], ar_h[i][:n2] + ak_v[i]) for i in pairs]
    uv = [jnp.concatenate([u_st[i], v_st[i]], axis=0).astype(BF16) for i in pairs]
    y_st = [ar_h[i][n2:] + _nn_dot(q[i][n2:, :] * jnp.concatenate([incl, incl], axis=1), uv[i]) for i in pairs]
    ys = [y_st[i][:c] + y_st[i][c:] for i in pairs]
    ht_new = [(hts[i] + _tn_dot(uv[i], bk[i])) * p_last[i] for i in pairs]
    return ys, ht_new


def _rwkv_scan_body(r_ref, ld_ref, k_ref, v_ref, kk_ref, b_ref, bonus_ref, g_ref, gw_ref, gb_ref, ones_ref,
                    o_ref, ht_ref):
    c = r_ref.shape[1]
    n2 = 2 * c
    lanes = V7X_LANES

    @pl.when(pl.program_id(1) == 0)
    def _():
        ht_ref[...] = jnp.zeros_like(ht_ref)

    lane = lax.broadcasted_iota(jnp.int32, (1, lanes), 1)
    m0 = jnp.where(lane < B_HEAD_DIM, 1.0, 0.0)
    m1 = 1.0 - m0
    row = lax.broadcasted_iota(jnp.int32, (n2, n2), 0)
    col = lax.broadcasted_iota(jnp.int32, (n2, n2), 1)
    log2c = c.bit_length() - 1
    same = jnp.right_shift(row, log2c) == jnp.right_shift(col, log2c)
    strict = jnp.where(same & (row > col), 1.0, 0.0)
    incl = jnp.where(same & (row >= col), 1.0, 0.0)
    eye = jnp.where(row == col, 1.0, 0.0)
    levels = []
    for lh in range(log2c):
        blk = jnp.right_shift(row, lh + 1) == jnp.right_shift(col, lh + 1)
        lower_left = (jnp.bitwise_and(jnp.right_shift(row, lh), 1) == 1) & \
                     (jnp.bitwise_and(jnp.right_shift(col, lh), 1) == 0)
        levels.append(jnp.where(blk & lower_left, 1.0, 0.0))
    masks = (m0, m1, strict, incl, eye, levels)

    where = [(bi, slice(i * lanes, (i + 1) * lanes))
             for bi in range(r_ref.shape[0]) for i in range(r_ref.shape[2] // lanes)]
    streams = [tuple(ref[bi, :, cs].astype(F32) for ref in (r_ref, ld_ref, k_ref, v_ref, kk_ref, b_ref))
               for bi, cs in where]
    ys, ht_new = _rwkv_chunk(streams, [ht_ref[i] for i in range(len(where))], masks)
    for i in range(len(where)):
        ht_ref[i] = ht_new[i]
    inv_n = 1.0 / B_HEAD_DIM
    means = [jnp.dot(y.astype(BF16), ones_ref[...], preferred_element_type=F32) * inv_n for y in ys]
    ds = [y - mean for y, mean in zip(ys, means)]
    vrs = [jnp.dot((d * d).astype(BF16), ones_ref[...], preferred_element_type=F32) * inv_n for d in ds]
    for (bi, cs), d, var in zip(where, ds, vrs):
        yn = d * lax.rsqrt(var + GN_EPS) * gw_ref[:, cs] + gb_ref[:, cs]
        o_ref[bi, :, cs] = ((yn + bonus_ref[bi, :, cs].astype(F32)) * g_ref[bi, :, cs].astype(F32)
                            ).astype(o_ref.dtype)


def rwkv_scan(r, ld, k, v, kk, b, bonus, g, gn_w, gn_b, name):
    bsz, s, bw = r.shape
    c = RWKV_CHUNK
    nb = _pick(bsz, (RWKV_SEQS_PER_STEP, 1))
    assert s % c == 0 and c & (c - 1) == 0 and 2 * B_HEAD_DIM == V7X_LANES
    tok = pl.BlockSpec((nb, c, bw), lambda bi, i: (bi, i, 0))
    par = pl.BlockSpec((1, bw), lambda bi, i: (0, 0))
    chains = nb * (bw // V7X_LANES)
    est = (18 * nb * _nbytes((c, bw), F32) + chains * _nbytes((V7X_LANES, V7X_LANES), F32)
           + 8 * chains * _nbytes((4 * c, 4 * c), F32))
    return pl.pallas_call(
        _rwkv_scan_body,
        grid=(bsz // nb, s // c),
        in_specs=[tok] * 8 + [par] * 2 + [pl.BlockSpec((V7X_LANES, V7X_LANES), lambda bi, i: (0, 0))],
        out_specs=tok,
        out_shape=jax.ShapeDtypeStruct((bsz, s, bw), BF16),
        scratch_shapes=[pltpu.VMEM((chains, V7X_LANES, V7X_LANES), F32)],
        compiler_params=pltpu.CompilerParams(
            dimension_semantics=("parallel", "arbitrary"), vmem_limit_bytes=_vmem_limit(est)),
        name=name,
    )(r, ld, k, v, kk, b, bonus, g, gn_w.reshape(1, bw), gn_b.reshape(1, bw), _head_pair_ones())


def _lora_proj_weight(w_in, has_vres):
    o = 3 * A_WIDTH + 3 * B_WIDTH
    d = w_in.shape[0]
    wz = jnp.zeros((d, Z_WIDTH), w_in.dtype)
    wz = wz.at[:, ZW_OFF:ZW_OFF + LORA_W].set(w_in[:, o:o + LORA_W])
    o += LORA_W
    wz = wz.at[:, ZA_OFF:ZA_OFF + LORA_A].set(w_in[:, o:o + LORA_A])
    o += LORA_A
    wz = wz.at[:, ZG_OFF:ZG_OFF + LORA_G].set(w_in[:, o:o + LORA_G])
    o += LORA_G
    if has_vres:
        wz = wz.at[:, ZV_OFF:ZV_OFF + LORA_V].set(w_in[:, o:o + LORA_V])
    return wz


def moba_rwkv_layer(x, b, s, li, norm_mix, w_in, q_gain, k_gain, shift_mu, w0, w_lora, a0, a_lora, g_lora,
                    k_k, k_a, r_k, gn_w, gn_b, w_out, v0, v_lora, v_first):
    has_vres = v_lora is not None
    tag = "l%d_" % li
    n_main = 3 * A_WIDTH + 3 * B_WIDTH
    proj_a, proj_b, pz = norm_matmul_split(
        x, norm_mix, w_in.astype(BF16), n_main, 3 * A_WIDTH, _lora_proj_weight(w_in, has_vres).astype(BF16),
        F32, BF16, F32, tag + "in_proj")
    pz = pz.reshape(b, s, Z_WIDTH)

    ya_lo, ya_hi = moba_attention(proj_a.reshape(b, s, 3 * A_WIDTH), q_gain, k_gain, tag + "moba")

    r, ld, k, v, kk, kb, bonus, g = rwkv_prep(proj_b.reshape(b, s, 3 * B_WIDTH), pz, shift_mu, w0, w_lora, a0,
                                              a_lora, g_lora, k_k, k_a, r_k.reshape(-1), v0, v_lora, v_first,
                                              tag + "rwkv_prep")
    if not has_vres:
        v_first = v
    y_b = rwkv_scan(r, ld, k, v, kk, kb, bonus, g, gn_w, gn_b, tag + "rwkv_scan")

    x = mix_out_proj(ya_lo.reshape(b * s // 2, A_WIDTH), ya_hi.reshape(b * s // 2, A_WIDTH),
                     y_b.reshape(b * s, B_WIDTH), w_out.astype(BF16), x, s, tag + "out_proj")
    return x, v_first


def short_conv_layer(x, b, s, li, norm_mix, conv_in, conv_w, conv_out):
    tag = "l%d_" % li
    m = norm_matmul_conv(x, norm_mix, conv_in.astype(BF16), conv_w, s, 3, _short_conv_combine, tag + "conv_in")
    return matmul_residual(m, conv_out.astype(BF16), x, tag + "conv_out")


def conv_ffn(x, b, s, li, norm_ffn, ffn_up, ffn_conv, ffn_down):
    tag = "l%d_" % li
    m = norm_matmul_conv(x, norm_ffn, ffn_up.astype(BF16), ffn_conv, s, 2, _ffn_combine, tag + "ffn_up")
    return matmul_residual(m, ffn_down.astype(BF16), x, tag + "ffn_down")


def kernel(x, l0_norm_mix, l0_w_in, l0_q_gain, l0_k_gain, l0_shift_mu, l0_w0, l0_w_lora, l0_a0, l0_a_lora, l0_g_lora, l0_k_k, l0_k_a, l0_r_k, l0_gn_w, l0_gn_b, l0_w_out, l0_norm_ffn, l0_ffn_up, l0_ffn_conv, l0_ffn_down, l1_norm_mix, l1_conv_in, l1_conv_w, l1_conv_out, l1_norm_ffn, l1_ffn_up, l1_ffn_conv, l1_ffn_down, l2_norm_mix, l2_w_in, l2_q_gain, l2_k_gain, l2_shift_mu, l2_w0, l2_w_lora, l2_a0, l2_a_lora, l2_g_lora, l2_k_k, l2_k_a, l2_r_k, l2_gn_w, l2_gn_b, l2_w_out, l2_v0, l2_v_lora, l2_norm_ffn, l2_ffn_up, l2_ffn_conv, l2_ffn_down, l3_norm_mix, l3_conv_in, l3_conv_w, l3_conv_out, l3_norm_ffn, l3_ffn_up, l3_ffn_conv, l3_ffn_down):
    b, s, d = x.shape
    h = x.reshape(b * s, d)

    h, v_first = moba_rwkv_layer(
        h, b, s, 0, l0_norm_mix, l0_w_in, l0_q_gain, l0_k_gain, l0_shift_mu, l0_w0, l0_w_lora, l0_a0,
        l0_a_lora, l0_g_lora, l0_k_k, l0_k_a, l0_r_k, l0_gn_w, l0_gn_b, l0_w_out, None, None, None)
    h = conv_ffn(h, b, s, 0, l0_norm_ffn, l0_ffn_up, l0_ffn_conv, l0_ffn_down)

    h = short_conv_layer(h, b, s, 1, l1_norm_mix, l1_conv_in, l1_conv_w, l1_conv_out)
    h = conv_ffn(h, b, s, 1, l1_norm_ffn, l1_ffn_up, l1_ffn_conv, l1_ffn_down)

    h, _ = moba_rwkv_layer(
        h, b, s, 2, l2_norm_mix, l2_w_in, l2_q_gain, l2_k_gain, l2_shift_mu, l2_w0, l2_w_lora, l2_a0,
        l2_a_lora, l2_g_lora, l2_k_k, l2_k_a, l2_r_k, l2_gn_w, l2_gn_b, l2_w_out, l2_v0, l2_v_lora, v_first)
    h = conv_ffn(h, b, s, 2, l2_norm_ffn, l2_ffn_up, l2_ffn_conv, l2_ffn_down)

    h = short_conv_layer(h, b, s, 3, l3_norm_mix, l3_conv_in, l3_conv_w, l3_conv_out)
    h = conv_ffn(h, b, s, 3, l3_norm_ffn, l3_ffn_up, l3_ffn_conv, l3_ffn_down)
    return h.reshape(b, s, d)
```

```python
import functools

import jax
import jax.numpy as jnp
from jax import lax
from jax.experimental import pallas as pl
from jax.experimental.pallas import tpu as pltpu

F32 = jnp.float32
BF16 = jnp.bfloat16

A_HEADS = 8
A_HEAD_DIM = 128
A_WIDTH = A_HEADS * A_HEAD_DIM
MOBA_BLOCK = 256
MOBA_TOPK = 3
MOBA_HEADS_PER_STEP = 4
MOBA_DENOM_ROWS = 16
B_HEADS = 16
B_HEAD_DIM = 64
B_WIDTH = B_HEADS * B_HEAD_DIM
RWKV_CHUNK = 64
RWKV_SEQS_PER_STEP = 2
LORA_W = 64
LORA_A = 64
LORA_G = 160
LORA_V = 32
CONV_WIDTH = 3
RMS_EPS = 1e-6
GN_EPS = 64e-5
NEG_INF = -1e30
LOG2E = 1.4426950408889634

V7X_LANES = 128
V7X_SUBLANES = 8
V7X_VMEM_BYTES = 64 * 1024 * 1024
V7X_VMEM_CAP = V7X_VMEM_BYTES - 8 * 1024 * 1024
SPILL_ALLOWANCE = 4 * 1024 * 1024
CONV_ROW_CHUNK = 256

ZW_OFF, ZW_PAD = 0, 128
ZA_OFF, ZA_PAD = 128, 128
ZG_OFF, ZG_PAD = 256, 256
ZV_OFF, ZV_PAD = 512, 128
Z_WIDTH = 640


def _vmem_limit(nbytes):
    return int(min(max(nbytes + SPILL_ALLOWANCE, 16 * 1024 * 1024), V7X_VMEM_CAP))


def _nbytes(shape, dtype):
    n = 1
    for s in shape:
        n *= s
    return n * jnp.dtype(dtype).itemsize


def _pick(n, prefs):
    for p in prefs:
        if n % p == 0:
            return p
    return n


def _norm_matmul_split_body(x_ref, g_ref, w_ref, wz_ref, oa_ref, ob_ref, oz_ref, xn_ref, *, na, nt):
    j = pl.program_id(1)

    @pl.when(j == 0)
    def _():
        x = x_ref[...]
        ms = jnp.mean(x * x, axis=-1, keepdims=True)
        xn = (x * lax.rsqrt(ms + RMS_EPS) * g_ref[...]).astype(BF16)
        xn_ref[...] = xn
        oz_ref[...] = jnp.dot(xn, wz_ref[...], preferred_element_type=F32).astype(oz_ref.dtype)

    @pl.when(j > 0)
    def _():
        h = jnp.dot(xn_ref[...], w_ref[...], preferred_element_type=F32)

        @pl.when(j <= na)
        def _():
            oa_ref[...] = h.astype(oa_ref.dtype)

        @pl.when(j > na)
        def _():
            ob_ref[...] = h.astype(ob_ref.dtype)


def norm_matmul_split(x, gain, w, n, n_a, wz, dtype_a, dtype_b, dtype_z, name):
    m, k = x.shape
    nz = wz.shape[1]
    tm = _pick(m, (1024, 512, 256, 128))
    tn = _pick(n_a, (1024, 512, 256, 128))
    assert n % tn == 0
    na = n_a // tn
    nt = n // tn
    est = (2 * _nbytes((tm, k), F32) + _nbytes((tm, k), BF16) + 2 * _nbytes((k, tn + nz), BF16)
           + 8 * _nbytes((tm, tn), F32) + 3 * _nbytes((tm, nz), F32) + _nbytes((tm, k), F32))
    return pl.pallas_call(
        functools.partial(_norm_matmul_split_body, na=na, nt=nt),
        grid=(m // tm, nt + 1),
        in_specs=[
            pl.BlockSpec((tm, k), lambda i, j: (i, 0)),
            pl.BlockSpec((1, k), lambda i, j: (0, 0)),
            pl.BlockSpec((k, tn), lambda i, j: (0, jnp.maximum(j - 1, 0))),
            pl.BlockSpec((k, nz), lambda i, j: (0, 0)),
        ],
        out_specs=[
            pl.BlockSpec((tm, tn), lambda i, j: (i, jnp.clip(j - 1, 0, na - 1))),
            pl.BlockSpec((tm, tn), lambda i, j: (i, jnp.clip(j - 1 - na, 0, nt - na - 1))),
            pl.BlockSpec((tm, nz), lambda i, j: (i, 0)),
        ],
        out_shape=[jax.ShapeDtypeStruct((m, n_a), dtype_a), jax.ShapeDtypeStruct((m, n - n_a), dtype_b),
                   jax.ShapeDtypeStruct((m, nz), dtype_z)],
        scratch_shapes=[pltpu.VMEM((tm, k), BF16)],
        compiler_params=pltpu.CompilerParams(
            dimension_semantics=("arbitrary", "arbitrary"), vmem_limit_bytes=_vmem_limit(est)),
        name=name,
    )(x, gain.reshape(1, k), w, wz)


def _conv3_ext(u, w):
    return w[0:1, :] * pltpu.roll(u, 2, 0) + w[1:2, :] * pltpu.roll(u, 1, 0) + w[2:3, :] * u


def _norm_matmul_conv_body(x_ref, g_ref, *refs, n_groups, n_taps, tiles_per_seq, combine):
    w_refs = refs[:n_groups]
    cw_refs = refs[n_groups:n_groups + n_taps]
    o_ref, xn_ref, carry_ref = refs[n_groups + n_taps:]
    i = pl.program_id(0)
    j = pl.program_id(1)
    tm, tn = o_ref.shape
    sub = V7X_SUBLANES
    rc = min(tm, CONV_ROW_CHUNK)

    @pl.when(j == 0)
    def _():
        x = x_ref[...]
        ms = jnp.mean(x * x, axis=-1, keepdims=True)
        xn_ref[...] = (x * lax.rsqrt(ms + RMS_EPS) * g_ref[...]).astype(BF16)

    @pl.when(i % tiles_per_seq == 0)
    def _():
        carry_ref[j] = jnp.zeros(carry_ref.shape[1:], F32)

    w = jnp.concatenate([w_ref[...] for w_ref in w_refs], axis=1)
    h = jnp.dot(xn_ref[...], w, preferred_element_type=F32)
    taps = [cw_ref[...] for cw_ref in cw_refs]
    for c in range(tm // rc):
        r0 = c * rc
        if c == 0:
            ext = jnp.concatenate([carry_ref[j], h[:rc, :]], axis=0)
        else:
            ext = h[r0 - sub:r0 + rc, :]
        groups = [ext[:, g * tn:(g + 1) * tn] for g in range(n_groups)]
        o_ref[r0:r0 + rc, :] = combine(groups, taps)[sub:, :].astype(o_ref.dtype)
    carry_ref[j] = h[tm - sub:, :]


def _ffn_combine(groups, taps):
    return jax.nn.silu(_conv3_ext(groups[0], taps[0])) * _conv3_ext(groups[1], taps[1])


def _short_conv_combine(groups, taps):
    return groups[0] * _conv3_ext(groups[1] * groups[2], taps[0])


def norm_matmul_conv(x, gain, w, conv_w, seq, n_groups, combine, name):
    m, k = x.shape
    n = w.shape[1] // n_groups
    tm = _pick(seq, (1024, 512, 256, 128))
    tn = _pick(n, (512, 256, 128))
    nj = n // tn
    n_cw = conv_w.shape[1] // n
    est = (2 * _nbytes((tm, k), F32) + _nbytes((tm, k), BF16) + 3 * n_groups * _nbytes((k, tn), BF16)
           + 2 * _nbytes((tm, tn), BF16) + (n_groups + 2) * _nbytes((tm, tn), F32)
           + 8 * n_groups * _nbytes((CONV_ROW_CHUNK, tn), F32) + nj * n_groups * _nbytes((V7X_SUBLANES, tn), F32))
    w_specs = [pl.BlockSpec((k, tn), lambda i, j, g=g: (0, g * nj + j)) for g in range(n_groups)]
    cw_specs = [pl.BlockSpec((CONV_WIDTH, tn), lambda i, j, g=g: (0, g * nj + j)) for g in range(n_cw)]
    return pl.pallas_call(
        functools.partial(_norm_matmul_conv_body, n_groups=n_groups, n_taps=n_cw, tiles_per_seq=seq // tm,
                          combine=combine),
        grid=(m // tm, nj),
        in_specs=[pl.BlockSpec((tm, k), lambda i, j: (i, 0)), pl.BlockSpec((1, k), lambda i, j: (0, 0))]
        + w_specs + cw_specs,
        out_specs=pl.BlockSpec((tm, tn), lambda i, j: (i, j)),
        out_shape=jax.ShapeDtypeStruct((m, n), BF16),
        scratch_shapes=[pltpu.VMEM((tm, k), BF16),
                        pltpu.VMEM((nj, V7X_SUBLANES, n_groups * tn), F32)],
        compiler_params=pltpu.CompilerParams(
            dimension_semantics=("arbitrary", "arbitrary"), vmem_limit_bytes=_vmem_limit(est)),
        name=name,
    )(x, gain.reshape(1, k), *([w] * n_groups), *([conv_w] * n_cw))


def _matmul_res_body(a_ref, w_ref, r_ref, o_ref):
    o_ref[...] = r_ref[...] + jnp.dot(a_ref[...], w_ref[...], preferred_element_type=F32)


def matmul_residual(a, w, res, name):
    m, k = a.shape
    n = w.shape[1]
    tm = _pick(m, (1024, 512, 256, 128))
    tn = _pick(n, (512, 256, 128)) if k > 4096 else _pick(n, (1024, 512, 256, 128))
    est = (2 * _nbytes((tm, k), BF16) + 2 * _nbytes((k, tn), BF16) + 6 * _nbytes((tm, tn), F32))
    return pl.pallas_call(
        _matmul_res_body,
        grid=(m // tm, n // tn),
        in_specs=[
            pl.BlockSpec((tm, k), lambda i, j: (i, 0)),
            pl.BlockSpec((k, tn), lambda i, j: (0, j)),
            pl.BlockSpec((tm, tn), lambda i, j: (i, j)),
        ],
        out_specs=pl.BlockSpec((tm, tn), lambda i, j: (i, j)),
        out_shape=jax.ShapeDtypeStruct((m, n), F32),
        compiler_params=pltpu.CompilerParams(
            dimension_semantics=("parallel", "parallel"), vmem_limit_bytes=_vmem_limit(est)),
        name=name,
    )(a, w, res)


def _mix_out_body(ylo_ref, yhi_ref, yb_ref, wa_ref, wb_ref, r_ref, o_ref):
    half = ylo_ref.shape[0]
    for rows, ya_ref in ((slice(0, half), ylo_ref), (slice(half, 2 * half), yhi_ref)):
        acc = jnp.dot(ya_ref[...], wa_ref[...], preferred_element_type=F32)
        acc = acc + jnp.dot(yb_ref[rows, :], wb_ref[...], preferred_element_type=F32)
        o_ref[rows, :] = r_ref[rows, :] + acc


def mix_out_proj(ya_lo, ya_hi, yb, w_out, res, seq, name):
    m, kb = yb.shape
    ka = ya_lo.shape[1]
    n = w_out.shape[1]
    half = seq // 2
    tn = _pick(n, (512, 256, 128))
    est = (4 * _nbytes((half, ka), BF16) + 2 * _nbytes((seq, kb), BF16) + 2 * _nbytes((ka + kb, tn), BF16)
           + 5 * _nbytes((seq, tn), F32))
    return pl.pallas_call(
        _mix_out_body,
        grid=(m // seq, n // tn),
        in_specs=[
            pl.BlockSpec((half, ka), lambda i, j: (i, 0)),
            pl.BlockSpec((half, ka), lambda i, j: (i, 0)),
            pl.BlockSpec((seq, kb), lambda i, j: (i, 0)),
            pl.BlockSpec((ka, tn), lambda i, j: (0, j)),
            pl.BlockSpec((kb, tn), lambda i, j: (0, j)),
            pl.BlockSpec((seq, tn), lambda i, j: (i, j)),
        ],
        out_specs=pl.BlockSpec((seq, tn), lambda i, j: (i, j)),
        out_shape=jax.ShapeDtypeStruct((m, n), F32),
        compiler_params=pltpu.CompilerParams(
            dimension_semantics=("parallel", "parallel"), vmem_limit_bytes=_vmem_limit(est)),
        name=name,
    )(ya_lo, ya_hi, yb, w_out[:ka], w_out[ka:], res)


def _head_rms(x, gain):
    return x * lax.rsqrt(jnp.mean(x * x, axis=-1, keepdims=True) + RMS_EPS) * gain


def _moba_query_block(ii, q, q_gain, slope, rel_t, kn_ref, vt_ref, km_ref):
    bs = MOBA_BLOCK
    hd = A_HEAD_DIM
    qn = _head_rms(q, q_gain)
    qs = (qn * (hd ** -0.5 * LOG2E)).astype(BF16)
    slope2 = slope * LOG2E
    bias_rel = slope2 * rel_t
    gt = lax.dot_general(km_ref[...], qn, (((1,), (1,)), ((), ())),
                         precision=lax.Precision.HIGHEST, preferred_element_type=F32)
    blk = lax.broadcasted_iota(jnp.int32, gt.shape, 0)
    gm = jnp.where(blk < ii, gt, NEG_INF)
    scores = []
    m_all = None
    for n in range(ii + 1):
        s = lax.dot_general(kn_ref[n * bs:(n + 1) * bs, :], qs, (((1,), (1,)), ((), ())),
                            preferred_element_type=F32) - bias_rel
        if n == ii:
            s = jnp.where(rel_t >= 0, s, NEG_INF)
            shift = jnp.zeros((1, bs), F32)
        else:
            g_n = gm[n:n + 1, :]
            beats = jnp.where((gm > g_n) | ((gm == g_n) & (blk < n)), 1.0, 0.0)
            chosen = jnp.sum(beats, axis=0, keepdims=True) < float(MOBA_TOPK)
            shift = jnp.where(chosen, slope2 * float((ii - n) * bs), -NEG_INF)
        m_blk = jnp.max(s, axis=0, keepdims=True) - shift
        m_all = m_blk if m_all is None else jnp.maximum(m_all, m_blk)
        scores.append((s, shift))
    acc = None
    for n, (s, shift) in enumerate(scores):
        p = jnp.exp2(s - (m_all + shift)).astype(BF16)
        pv = jnp.dot(vt_ref[:, n * bs:(n + 1) * bs], p, preferred_element_type=F32)
        acc = pv if acc is None else acc + pv
    return (acc[:hd] / acc[hd:hd + 1]).T


def _moba_body(qlo_ref, qhi_ref, k_ref, v_ref, qg_ref, kg_ref, olo_ref, ohi_ref, kn_ref, vt_ref, km_ref,
               *, n_heads, n_blocks):
    bs = MOBA_BLOCK
    hd = A_HEAD_DIM
    hp = pl.program_id(1)
    j = pl.program_id(2)

    @pl.when(j == 0)
    def _():
        km_ref[...] = jnp.zeros_like(km_ref)
        for e in range(MOBA_HEADS_PER_STEP):
            cols = slice(e * hd, (e + 1) * hd)
            for nb in range(n_blocks):
                rows = slice(nb * bs, (nb + 1) * bs)
                kn = _head_rms(k_ref[0, rows, cols], kg_ref[...])
                kn_ref[e, rows, :] = kn.astype(BF16)
                km_ref[e, nb:nb + 1, :] = jnp.mean(kn, axis=0, keepdims=True)
                vt_ref[e, :hd, rows] = v_ref[0, rows, cols].T.astype(BF16)
            vt_ref[e, hd:, :] = jnp.ones((MOBA_DENOM_ROWS, vt_ref.shape[2]), BF16)

    rel_t = (lax.broadcasted_iota(jnp.int32, (bs, bs), 1)
             - lax.broadcasted_iota(jnp.int32, (bs, bs), 0)).astype(F32)

    for jj in range(n_blocks // 2):
        @pl.when(j == jj)
        def _(jj=jj):
            for e in range(MOBA_HEADS_PER_STEP):
                cols = slice(e * hd, (e + 1) * hd)
                head = (hp * MOBA_HEADS_PER_STEP + e + 1).astype(F32)
                slope = jnp.exp2(jnp.full((1, bs), -8.0 / n_heads, F32) * head)
                for q_ref, o_ref, ii in ((qlo_ref, olo_ref, jj), (qhi_ref, ohi_ref, n_blocks - 1 - jj)):
                    o = _moba_query_block(ii, q_ref[0, :, cols], qg_ref[...], slope, rel_t,
                                          kn_ref.at[e], vt_ref.at[e], km_ref.at[e])
                    o_ref[0, :, cols] = o.astype(o_ref.dtype)


def moba_attention(proj, q_gain, k_gain, name):
    b, s, _ = proj.shape
    assert s % MOBA_BLOCK == 0
    nb = s // MOBA_BLOCK
    assert nb % 2 == 0 and A_HEADS % MOBA_HEADS_PER_STEP == 0
    nbp = -(-nb // V7X_SUBLANES) * V7X_SUBLANES
    hd = A_HEAD_DIM
    hw = MOBA_HEADS_PER_STEP * hd
    groups = A_HEADS // MOBA_HEADS_PER_STEP
    half = nb // 2
    est = (8 * _nbytes((MOBA_BLOCK, hw), F32) + 4 * _nbytes((s, hw), F32) + 2 * _nbytes((s, hw), BF16)
           + MOBA_HEADS_PER_STEP * (4 * nb + 16) * _nbytes((MOBA_BLOCK, MOBA_BLOCK), F32))
    o_lo, o_hi = pl.pallas_call(
        functools.partial(_moba_body, n_heads=A_HEADS, n_blocks=nb),
        grid=(b, groups, half),
        in_specs=[
            pl.BlockSpec((1, MOBA_BLOCK, hw), lambda bi, g, j: (bi, j, g)),
            pl.BlockSpec((1, MOBA_BLOCK, hw), lambda bi, g, j: (bi, nb - 1 - j, g)),
            pl.BlockSpec((1, s, hw), lambda bi, g, j: (bi, 0, groups + g)),
            pl.BlockSpec((1, s, hw), lambda bi, g, j: (bi, 0, 2 * groups + g)),
            pl.BlockSpec((1, hd), lambda bi, g, j: (0, 0)),
            pl.BlockSpec((1, hd), lambda bi, g, j: (0, 0)),
        ],
        out_specs=[
            pl.BlockSpec((1, MOBA_BLOCK, hw), lambda bi, g, j: (bi, j, g)),
            pl.BlockSpec((1, MOBA_BLOCK, hw), lambda bi, g, j: (bi, half - 1 - j, g)),
        ],
        out_shape=[jax.ShapeDtypeStruct((b, s // 2, A_WIDTH), BF16)] * 2,
        scratch_shapes=[
            pltpu.VMEM((MOBA_HEADS_PER_STEP, s, hd), BF16),
            pltpu.VMEM((MOBA_HEADS_PER_STEP, hd + MOBA_DENOM_ROWS, s), BF16),
            pltpu.VMEM((MOBA_HEADS_PER_STEP, nbp, hd), F32),
        ],
        compiler_params=pltpu.CompilerParams(
            dimension_semantics=("parallel", "parallel", "arbitrary"), vmem_limit_bytes=_vmem_limit(est)),
        name=name,
    )(proj, proj, proj, proj, q_gain.reshape(1, hd), k_gain.reshape(1, hd))
    return o_lo, o_hi


def _token_shift(p, prev_last, mu):
    rolled = pltpu.roll(p, 1, 0)
    row = lax.broadcasted_iota(jnp.int32, p.shape, 0)
    p_prev = jnp.where(row == 0, prev_last, rolled)
    return p + mu * (p_prev - p)


def _head_sums(x, ones_ref):
    tiles = [jnp.dot(x[:, c:c + V7X_LANES].astype(BF16), ones_ref[...], preferred_element_type=F32)
             for c in range(0, x.shape[1], V7X_LANES)]
    return jnp.concatenate(tiles, axis=1)


def _rwkv_prep_body(*refs, has_vres):
    if has_vres:
        (pm_ref, pz_ref, mum_ref, muz_ref, w0_ref, wl_ref, a0_ref, al_ref, gl_ref, kk_ref, ka_ref, rk_ref,
         ones_ref, v0_ref, vl_ref, vf_ref,
         r_ref, ld_ref, k_ref, v_ref, kn_ref, b_ref, bonus_ref, g_ref, cm_ref, cz_ref) = refs
    else:
        (pm_ref, pz_ref, mum_ref, muz_ref, w0_ref, wl_ref, a0_ref, al_ref, gl_ref, kk_ref, ka_ref, rk_ref,
         ones_ref,
         r_ref, ld_ref, k_ref, v_ref, kn_ref, b_ref, bonus_ref, g_ref, cm_ref, cz_ref) = refs
    tm = pm_ref.shape[1]
    bw = B_WIDTH

    @pl.when(pl.program_id(1) == 0)
    def _():
        cm_ref[...] = jnp.zeros_like(cm_ref)
        cz_ref[...] = jnp.zeros_like(cz_ref)

    pm = pm_ref[0].astype(F32)
    pz = pz_ref[0]
    last = V7X_SUBLANES - 1
    pms = _token_shift(pm, cm_ref[last:last + 1, :], mum_ref[...])
    pzs = _token_shift(pz, cz_ref[last:last + 1, :], muz_ref[...])
    cm_ref[...] = pm[tm - V7X_SUBLANES:, :]
    cz_ref[...] = pz[tm - V7X_SUBLANES:, :]

    r = pms[:, :bw]
    k = pms[:, bw:2 * bw]
    v = pms[:, 2 * bw:3 * bw]

    z_w = jnp.tanh(pzs[:, ZW_OFF:ZW_OFF + ZW_PAD]).astype(BF16)
    w = w0_ref[...] + jnp.dot(z_w, wl_ref[...], preferred_element_type=F32)
    w = -jax.nn.softplus(-w) - 0.5
    ld_ref[0] = -jnp.exp(w)

    z_a = pzs[:, ZA_OFF:ZA_OFF + ZA_PAD].astype(BF16)
    a = jax.nn.sigmoid(a0_ref[...] + jnp.dot(z_a, al_ref[...], preferred_element_type=F32))

    z_g = jax.nn.sigmoid(pzs[:, ZG_OFF:ZG_OFF + ZG_PAD]).astype(BF16)
    g_ref[0] = jnp.dot(z_g, gl_ref[...], preferred_element_type=F32).astype(g_ref.dtype)

    if has_vres:
        z_v = pzs[:, ZV_OFF:ZV_OFF + ZV_PAD].astype(BF16)
        mix = jax.nn.sigmoid(v0_ref[...] + jnp.dot(z_v, vl_ref[...], preferred_element_type=F32))
        v = v + (vf_ref[0].astype(F32) - v) * mix

    kk = k * kk_ref[...]
    kk = kk * jnp.minimum(lax.rsqrt(_head_sums(kk * kk, ones_ref)), 1e12)
    k = k * (1.0 + (a - 1.0) * ka_ref[...])
    r_ref[0] = r.astype(r_ref.dtype)
    k_ref[0] = k.astype(k_ref.dtype)
    v_ref[0] = v.astype(v_ref.dtype)
    kn_ref[0] = kk.astype(kn_ref.dtype)
    b_ref[0] = (kk * a).astype(b_ref.dtype)
    bonus_ref[0] = (_head_sums(r * k * rk_ref[...], ones_ref) * v).astype(bonus_ref.dtype)


def _pad_rows(w, rows):
    return jnp.pad(w, ((0, rows - w.shape[0]), (0, 0)))


def _head_pair_ones():
    head = jnp.arange(V7X_LANES) // B_HEAD_DIM
    return (head[:, None] == head[None, :]).astype(BF16)


def rwkv_prep(proj, pz, shift_mu, w0, w_lora, a0, a_lora, g_lora, k_k, k_a, r_k, v0, v_lora, v_first, name):
    b, s, main_blk = proj.shape
    bw = B_WIDTH
    has_vres = v_lora is not None
    tm = _pick(s, (256, 128))
    assert main_blk == 3 * bw

    mu_main = shift_mu[:3 * bw].reshape(1, 3 * bw)
    lora_mu = shift_mu[3 * bw:]
    mu_z = jnp.zeros((Z_WIDTH,), F32)
    mu_z = mu_z.at[ZW_OFF:ZW_OFF + LORA_W].set(lora_mu[:LORA_W])
    mu_z = mu_z.at[ZA_OFF:ZA_OFF + LORA_A].set(lora_mu[LORA_W:LORA_W + LORA_A])
    mu_z = mu_z.at[ZG_OFF:ZG_OFF + LORA_G].set(lora_mu[LORA_W + LORA_A:LORA_W + LORA_A + LORA_G])
    if has_vres:
        mu_z = mu_z.at[ZV_OFF:ZV_OFF + LORA_V].set(lora_mu[LORA_W + LORA_A + LORA_G:])
    mu_z = mu_z.reshape(1, Z_WIDTH)

    row = lambda t: t.reshape(1, bw)
    const = lambda shape: pl.BlockSpec(shape, lambda bi, i: (0, 0))
    tok = pl.BlockSpec((1, tm, bw), lambda bi, i: (bi, i, 0))
    args = [proj, pz, mu_main, mu_z, row(w0), _pad_rows(w_lora, ZW_PAD).astype(BF16),
            row(a0), _pad_rows(a_lora, ZA_PAD).astype(BF16), _pad_rows(g_lora, ZG_PAD).astype(BF16),
            row(k_k), row(k_a), row(r_k), _head_pair_ones()]
    in_specs = [
        pl.BlockSpec((1, tm, main_blk), lambda bi, i: (bi, i, 0)),
        pl.BlockSpec((1, tm, Z_WIDTH), lambda bi, i: (bi, i, 0)),
        const((1, main_blk)), const((1, Z_WIDTH)),
        const((1, bw)), const((ZW_PAD, bw)), const((1, bw)), const((ZA_PAD, bw)), const((ZG_PAD, bw)),
        const((1, bw)), const((1, bw)), const((1, bw)), const((V7X_LANES, V7X_LANES)),
    ]
    if has_vres:
        args += [row(v0), _pad_rows(v_lora, ZV_PAD).astype(BF16), v_first]
        in_specs += [const((1, bw)), const((ZV_PAD, bw)), tok]
    out_dtypes = [BF16, F32, BF16, BF16, BF16, BF16, BF16, BF16]
    n_out = len(out_dtypes)
    est = (2 * _nbytes((tm, main_blk + Z_WIDTH), F32) + (2 * n_out + 6) * _nbytes((tm, bw), F32)
           + 6 * _nbytes((tm, main_blk), F32) + 4 * _nbytes((ZG_PAD + 3 * ZW_PAD, bw), BF16))
    return pl.pallas_call(
        functools.partial(_rwkv_prep_body, has_vres=has_vres),
        grid=(b, s // tm),
        in_specs=in_specs,
        out_specs=[tok] * n_out,
        out_shape=[jax.ShapeDtypeStruct((b, s, bw), dt) for dt in out_dtypes],
        scratch_shapes=[pltpu.VMEM((V7X_SUBLANES, main_blk), F32), pltpu.VMEM((V7X_SUBLANES, Z_WIDTH), F32)],
        compiler_params=pltpu.CompilerParams(
            dimension_semantics=("parallel", "arbitrary"), vmem_limit_bytes=_vmem_limit(est)),
        name=name,
    )(*args)


def _cumsum_rows(x):
    row = lax.broadcasted_iota(jnp.int32, x.shape, 0)
    sh = 1
    while sh < x.shape[0]:
        x = x + jnp.where(row >= sh, pltpu.roll(x, sh, 0), 0.0)
        sh *= 2
    return x


def _stack_heads(x, m0, m1):
    return jnp.concatenate([x * m0, x * m1], axis=0)


def _nt_dot(a, b):
    return lax.dot_general(a.astype(BF16), b.astype(BF16), (((1,), (1,)), ((), ())), preferred_element_type=F32)


def _tn_dot(a, b):
    return lax.dot_general(a.astype(BF16), b.astype(BF16), (((0,), (0,)), ((), ())), preferred_element_type=F32)


def _nn_dot(a, b):
    return jnp.dot(a.astype(BF16), b.astype(BF16), preferred_element_type=F32)


def _rwkv_chunk(streams, hts, masks):
    pairs = range(len(streams))
    c = streams[0][0].shape[0]
    n2 = 2 * c
    m0, m1, strict, incl, eye, levels = masks
    ar, bk, v_st, p_last = [], [], [], []
    for r, ld, k, v, kk, b in streams:
        cs = _cumsum_rows(ld)
        p = jnp.exp(cs)
        p_inv = jnp.exp(-cs)
        a_st = _stack_heads(-kk * jnp.exp(cs - ld), m0, m1)
        r_st = _stack_heads(r * p, m0, m1)
        b_st = _stack_heads(b * p_inv, m0, m1)
        k_st = _stack_heads(k * p_inv, m0, m1)
        ar.append(jnp.concatenate([a_st, r_st], axis=0).astype(BF16))
        bk.append(jnp.concatenate([b_st, k_st], axis=0).astype(BF16))
        v_st.append(_stack_heads(v, m0, m1))
        p_last.append(p[c - 1:c, :])

    q = [_nt_dot(ar[i], bk[i]) for i in pairs]
    l_ab = [q[i][:n2, :n2] * strict for i in pairs]
    t = [eye + l_ab[i] * levels[0] for i in pairs]
    for lvl in levels[1:]:
        tc = [_nn_dot(t[i], l_ab[i] * lvl) for i in pairs]
        t = [t[i] + _nn_dot(tc[i], t[i]) for i in pairs]

    ar_h = [_nt_dot(ar[i], hts[i]) for i in pairs]
    ak_v = [_nn_dot(q[i][:n2, n2:] * strict, v_st[i]) for i in pairs]
    u_st = [_nn_dot(t[i], ar_h[i][:n2] + ak_v[i]) for i in pairs]
    uv = [jnp.concatenate([u_st[i], v_st[i]], axis=0).astype(BF16) for i in pairs]
    y_st = [ar_h[i][n2:] + _nn_dot(q[i][n2:, :] * jnp.concatenate([incl, incl], axis=1), uv[i]) for i in pairs]
    ys = [y_st[i][:c] + y_st[i][c:] for i in pairs]
    ht_new = [(hts[i] + _tn_dot(uv[i], bk[i])) * p_last[i] for i in pairs]
    return ys, ht_new


def _rwkv_scan_body(r_ref, ld_ref, k_ref, v_ref, kk_ref, b_ref, bonus_ref, g_ref, gw_ref, gb_ref, ones_ref,
                    o_ref, ht_ref):
    c = r_ref.shape[1]
    n2 = 2 * c
    lanes = V7X_LANES

    @pl.when(pl.program_id(1) == 0)
    def _():
        ht_ref[...] = jnp.zeros_like(ht_ref)

    lane = lax.broadcasted_iota(jnp.int32, (1, lanes), 1)
    m0 = jnp.where(lane < B_HEAD_DIM, 1.0, 0.0)
    m1 = 1.0 - m0
    row = lax.broadcasted_iota(jnp.int32, (n2, n2), 0)
    col = lax.broadcasted_iota(jnp.int32, (n2, n2), 1)
    log2c = c.bit_length() - 1
    same = jnp.right_shift(row, log2c) == jnp.right_shift(col, log2c)
    strict = jnp.where(same & (row > col), 1.0, 0.0)
    incl = jnp.where(same & (row >= col), 1.0, 0.0)
    eye = jnp.where(row == col, 1.0, 0.0)
    levels = []
    for lh in range(log2c):
        blk = jnp.right_shift(row, lh + 1) == jnp.right_shift(col, lh + 1)
        lower_left = (jnp.bitwise_and(jnp.right_shift(row, lh), 1) == 1) & \
                     (jnp.bitwise_and(jnp.right_shift(col, lh), 1) == 0)
        levels.append(jnp.where(blk & lower_left, 1.0, 0.0))
    masks = (m0, m1, strict, incl, eye, levels)

    where = [(bi, slice(i * lanes, (i + 1) * lanes))
             for bi in range(r_ref.shape[0]) for i in range(r_ref.shape[2] // lanes)]
    streams = [tuple(ref[bi, :, cs].astype(F32) for ref in (r_ref, ld_ref, k_ref, v_ref, kk_ref, b_ref))
               for bi, cs in where]
    ys, ht_new = _rwkv_chunk(streams, [ht_ref[i] for i in range(len(where))], masks)
    for i in range(len(where)):
        ht_ref[i] = ht_new[i]
    inv_n = 1.0 / B_HEAD_DIM
    means = [jnp.dot(y.astype(BF16), ones_ref[...], preferred_element_type=F32) * inv_n for y in ys]
    ds = [y - mean for y, mean in zip(ys, means)]
    vrs = [jnp.dot((d * d).astype(BF16), ones_ref[...], preferred_element_type=F32) * inv_n for d in ds]
    for (bi, cs), d, var in zip(where, ds, vrs):
        yn = d * lax.rsqrt(var + GN_EPS) * gw_ref[:, cs] + gb_ref[:, cs]
        o_ref[bi, :, cs] = ((yn + bonus_ref[bi, :, cs].astype(F32)) * g_ref[bi, :, cs].astype(F32)
                            ).astype(o_ref.dtype)


def rwkv_scan(r, ld, k, v, kk, b, bonus, g, gn_w, gn_b, name):
    bsz, s, bw = r.shape
    c = RWKV_CHUNK
    nb = _pick(bsz, (RWKV_SEQS_PER_STEP, 1))
    assert s % c == 0 and c & (c - 1) == 0 and 2 * B_HEAD_DIM == V7X_LANES
    tok = pl.BlockSpec((nb, c, bw), lambda bi, i: (bi, i, 0))
    par = pl.BlockSpec((1, bw), lambda bi, i: (0, 0))
    chains = nb * (bw // V7X_LANES)
    est = (18 * nb * _nbytes((c, bw), F32) + chains * _nbytes((V7X_LANES, V7X_LANES), F32)
           + 8 * chains * _nbytes((4 * c, 4 * c), F32))
    return pl.pallas_call(
        _rwkv_scan_body,
        grid=(bsz // nb, s // c),
        in_specs=[tok] * 8 + [par] * 2 + [pl.BlockSpec((V7X_LANES, V7X_LANES), lambda bi, i: (0, 0))],
        out_specs=tok,
        out_shape=jax.ShapeDtypeStruct((bsz, s, bw), BF16),
        scratch_shapes=[pltpu.VMEM((chains, V7X_LANES, V7X_LANES), F32)],
        compiler_params=pltpu.CompilerParams(
            dimension_semantics=("parallel", "arbitrary"), vmem_limit_bytes=_vmem_limit(est)),
        name=name,
    )(r, ld, k, v, kk, b, bonus, g, gn_w.reshape(1, bw), gn_b.reshape(1, bw), _head_pair_ones())


def _lora_proj_weight(w_in, has_vres):
    o = 3 * A_WIDTH + 3 * B_WIDTH
    d = w_in.shape[0]
    wz = jnp.zeros((d, Z_WIDTH), w_in.dtype)
    wz = wz.at[:, ZW_OFF:ZW_OFF + LORA_W].set(w_in[:, o:o + LORA_W])
    o += LORA_W
    wz = wz.at[:, ZA_OFF:ZA_OFF + LORA_A].set(w_in[:, o:o + LORA_A])
    o += LORA_A
    wz = wz.at[:, ZG_OFF:ZG_OFF + LORA_G].set(w_in[:, o:o + LORA_G])
    o += LORA_G
    if has_vres:
        wz = wz.at[:, ZV_OFF:ZV_OFF + LORA_V].set(w_in[:, o:o + LORA_V])
    return wz


def moba_rwkv_layer(x, b, s, li, norm_mix, w_in, q_gain, k_gain, shift_mu, w0, w_lora, a0, a_lora, g_lora,
                    k_k, k_a, r_k, gn_w, gn_b, w_out, v0, v_lora, v_first):
    has_vres = v_lora is not None
    tag = "l%d_" % li
    n_main = 3 * A_WIDTH + 3 * B_WIDTH
    proj_a, proj_b, pz = norm_matmul_split(
        x, norm_mix, w_in.astype(BF16), n_main, 3 * A_WIDTH, _lora_proj_weight(w_in, has_vres).astype(BF16),
        F32, BF16, F32, tag + "in_proj")
    pz = pz.reshape(b, s, Z_WIDTH)

    ya_lo, ya_hi = moba_attention(proj_a.reshape(b, s, 3 * A_WIDTH), q_gain, k_gain, tag + "moba")

    r, ld, k, v, kk, kb, bonus, g = rwkv_prep(proj_b.reshape(b, s, 3 * B_WIDTH), pz, shift_mu, w0, w_lora, a0,
                                              a_lora, g_lora, k_k, k_a, r_k.reshape(-1), v0, v_lora, v_first,
                                              tag + "rwkv_prep")
    if not has_vres:
        v_first = v
    y_b = rwkv_scan(r, ld, k, v, kk, kb, bonus, g, gn_w, gn_b, tag + "rwkv_scan")

    x = mix_out_proj(ya_lo.reshape(b * s // 2, A_WIDTH), ya_hi.reshape(b * s // 2, A_WIDTH),
                     y_b.reshape(b * s, B_WIDTH), w_out.astype(BF16), x, s, tag + "out_proj")
    return x, v_first


def short_conv_layer(x, b, s, li, norm_mix, conv_in, conv_w, conv_out):
    tag = "l%d_" % li
    m = norm_matmul_conv(x, norm_mix, conv_in.astype(BF16), conv_w, s, 3, _short_conv_combine, tag + "conv_in")
    return matmul_residual(m, conv_out.astype(BF16), x, tag + "conv_out")


def conv_ffn(x, b, s, li, norm_ffn, ffn_up, ffn_conv, ffn_down):
    tag = "l%d_" % li
    m = norm_matmul_conv(x, norm_ffn, ffn_up.astype(BF16), ffn_conv, s, 2, _ffn_combine, tag + "ffn_up")
    return matmul_residual(m, ffn_down.astype(BF16), x, tag + "ffn_down")


def kernel(x, l0_norm_mix, l0_w_in, l0_q_gain, l0_k_gain, l0_shift_mu, l0_w0, l0_w_lora, l0_a0, l0_a_lora, l0_g_lora, l0_k_k, l0_k_a, l0_r_k, l0_gn_w, l0_gn_b, l0_w_out, l0_norm_ffn, l0_ffn_up, l0_ffn_conv, l0_ffn_down, l1_norm_mix, l1_conv_in, l1_conv_w, l1_conv_out, l1_norm_ffn, l1_ffn_up, l1_ffn_conv, l1_ffn_down, l2_norm_mix, l2_w_in, l2_q_gain, l2_k_gain, l2_shift_mu, l2_w0, l2_w_lora, l2_a0, l2_a_lora, l2_g_lora, l2_k_k, l2_k_a, l2_r_k, l2_gn_w, l2_gn_b, l2_w_out, l2_v0, l2_v_lora, l2_norm_ffn, l2_ffn_up, l2_ffn_conv, l2_ffn_down, l3_norm_mix, l3_conv_in, l3_conv_w, l3_conv_out, l3_norm_ffn, l3_ffn_up, l3_ffn_conv, l3_ffn_down):
    b, s, d = x.shape
    h = x.reshape(b * s, d)

    h, v_first = moba_rwkv_layer(
        h, b, s, 0, l0_norm_mix, l0_w_in, l0_q_gain, l0_k_gain, l0_shift_mu, l0_w0, l0_w_lora, l0_a0,
        l0_a_lora, l0_g_lora, l0_k_k, l0_k_a, l0_r_k, l0_gn_w, l0_gn_b, l0_w_out, None, None, None)
    h = conv_ffn(h, b, s, 0, l0_norm_ffn, l0_ffn_up, l0_ffn_conv, l0_ffn_down)

    h = short_conv_layer(h, b, s, 1, l1_norm_mix, l1_conv_in, l1_conv_w, l1_conv_out)
    h = conv_ffn(h, b, s, 1, l1_norm_ffn, l1_ffn_up, l1_ffn_conv, l1_ffn_down)

    h, _ = moba_rwkv_layer(
        h, b, s, 2, l2_norm_mix, l2_w_in, l2_q_gain, l2_k_gain, l2_shift_mu, l2_w0, l2_w_lora, l2_a0,
        l2_a_lora, l2_g_lora, l2_k_k, l2_k_a, l2_r_k, l2_gn_w, l2_gn_b, l2_w_out, l2_v0, l2_v_lora, v_first)
    h = conv_ffn(h, b, s, 2, l2_norm_ffn, l2_ffn_up, l2_ffn_conv, l2_ffn_down)

    h = short_conv_layer(h, b, s, 3, l3_norm_mix, l3_conv_in, l3_conv_w, l3_conv_out)
    h = conv_ffn(h, b, s, 3, l3_norm_ffn, l3_ffn_up, l3_ffn_conv, l3_ffn_down)
    return h.reshape(b, s, d)
```

```python
import functools

import jax
import jax.numpy as jnp
from jax import lax
from jax.experimental import pallas as pl
from jax.experimental.pallas import tpu as pltpu

F32 = jnp.float32
BF16 = jnp.bfloat16

A_HEADS = 8
A_HEAD_DIM = 128
A_WIDTH = A_HEADS * A_HEAD_DIM
MOBA_BLOCK = 256
MOBA_TOPK = 3
MOBA_HEADS_PER_STEP = 8
MOBA_DENOM_ROWS = 16
B_HEADS = 16
B_HEAD_DIM = 64
B_WIDTH = B_HEADS * B_HEAD_DIM
RWKV_CHUNK = 64
RWKV_SEQS_PER_STEP = 2
LORA_W = 64
LORA_A = 64
LORA_G = 160
LORA_V = 32
CONV_WIDTH = 3
RMS_EPS = 1e-6
GN_EPS = 64e-5
NEG_INF = -1e30
LOG2E = 1.4426950408889634

V7X_LANES = 128
V7X_SUBLANES = 8
V7X_VMEM_BYTES = 64 * 1024 * 1024
V7X_VMEM_CAP = V7X_VMEM_BYTES - 8 * 1024 * 1024
SPILL_ALLOWANCE = 4 * 1024 * 1024
CONV_ROW_CHUNK = 256

ZW_OFF, ZW_PAD = 0, 128
ZA_OFF, ZA_PAD = 128, 128
ZG_OFF, ZG_PAD = 256, 256
ZV_OFF, ZV_PAD = 512, 128
Z_WIDTH = 640


def _vmem_limit(nbytes):
    return int(min(max(nbytes + SPILL_ALLOWANCE, 16 * 1024 * 1024), V7X_VMEM_CAP))


def _nbytes(shape, dtype):
    n = 1
    for s in shape:
        n *= s
    return n * jnp.dtype(dtype).itemsize


def _pick(n, prefs):
    for p in prefs:
        if n % p == 0:
            return p
    return n


def _norm_matmul_split_body(x_ref, g_ref, w_ref, wz_ref, oa_ref, ob_ref, oz_ref, xn_ref, *, na, nt):
    j = pl.program_id(1)

    @pl.when(j == 0)
    def _():
        x = x_ref[...]
        ms = jnp.mean(x * x, axis=-1, keepdims=True)
        xn = (x * lax.rsqrt(ms + RMS_EPS) * g_ref[...]).astype(BF16)
        xn_ref[...] = xn
        oz_ref[...] = jnp.dot(xn, wz_ref[...], preferred_element_type=F32).astype(oz_ref.dtype)

    @pl.when(j > 0)
    def _():
        h = jnp.dot(xn_ref[...], w_ref[...], preferred_element_type=F32)

        @pl.when(j <= na)
        def _():
            oa_ref[...] = h.astype(oa_ref.dtype)

        @pl.when(j > na)
        def _():
            ob_ref[...] = h.astype(ob_ref.dtype)


def norm_matmul_split(x, gain, w, n, n_a, wz, dtype_a, dtype_b, dtype_z, name):
    m, k = x.shape
    nz = wz.shape[1]
    tm = _pick(m, (1024, 512, 256, 128))
    tn = _pick(n_a, (1024, 512, 256, 128))
    assert n % tn == 0
    na = n_a // tn
    nt = n // tn
    est = (2 * _nbytes((tm, k), F32) + _nbytes((tm, k), BF16) + 2 * _nbytes((k, tn + nz), BF16)
           + 8 * _nbytes((tm, tn), F32) + 3 * _nbytes((tm, nz), F32) + _nbytes((tm, k), F32))
    return pl.pallas_call(
        functools.partial(_norm_matmul_split_body, na=na, nt=nt),
        grid=(m // tm, nt + 1),
        in_specs=[
            pl.BlockSpec((tm, k), lambda i, j: (i, 0)),
            pl.BlockSpec((1, k), lambda i, j: (0, 0)),
            pl.BlockSpec((k, tn), lambda i, j: (0, jnp.maximum(j - 1, 0))),
            pl.BlockSpec((k, nz), lambda i, j: (0, 0)),
        ],
        out_specs=[
            pl.BlockSpec((tm, tn), lambda i, j: (i, jnp.clip(j - 1, 0, na - 1))),
            pl.BlockSpec((tm, tn), lambda i, j: (i, jnp.clip(j - 1 - na, 0, nt - na - 1))),
            pl.BlockSpec((tm, nz), lambda i, j: (i, 0)),
        ],
        out_shape=[jax.ShapeDtypeStruct((m, n_a), dtype_a), jax.ShapeDtypeStruct((m, n - n_a), dtype_b),
                   jax.ShapeDtypeStruct((m, nz), dtype_z)],
        scratch_shapes=[pltpu.VMEM((tm, k), BF16)],
        compiler_params=pltpu.CompilerParams(
            dimension_semantics=("arbitrary", "arbitrary"), vmem_limit_bytes=_vmem_limit(est)),
        name=name,
    )(x, gain.reshape(1, k), w, wz)


def _conv3_ext(u, w):
    return w[0:1, :] * pltpu.roll(u, 2, 0) + w[1:2, :] * pltpu.roll(u, 1, 0) + w[2:3, :] * u


def _norm_matmul_conv_body(x_ref, g_ref, *refs, n_groups, n_taps, tiles_per_seq, combine):
    w_refs = refs[:n_groups]
    cw_refs = refs[n_groups:n_groups + n_taps]
    o_ref, xn_ref, carry_ref = refs[n_groups + n_taps:]
    i = pl.program_id(0)
    j = pl.program_id(1)
    tm, tn = o_ref.shape
    sub = V7X_SUBLANES
    rc = min(tm, CONV_ROW_CHUNK)

    @pl.when(j == 0)
    def _():
        x = x_ref[...]
        ms = jnp.mean(x * x, axis=-1, keepdims=True)
        xn_ref[...] = (x * lax.rsqrt(ms + RMS_EPS) * g_ref[...]).astype(BF16)

    @pl.when(i % tiles_per_seq == 0)
    def _():
        carry_ref[j] = jnp.zeros(carry_ref.shape[1:], F32)

    w = jnp.concatenate([w_ref[...] for w_ref in w_refs], axis=1)
    h = jnp.dot(xn_ref[...], w, preferred_element_type=F32)
    taps = [cw_ref[...] for cw_ref in cw_refs]
    for c in range(tm // rc):
        r0 = c * rc
        if c == 0:
            ext = jnp.concatenate([carry_ref[j], h[:rc, :]], axis=0)
        else:
            ext = h[r0 - sub:r0 + rc, :]
        groups = [ext[:, g * tn:(g + 1) * tn] for g in range(n_groups)]
        o_ref[r0:r0 + rc, :] = combine(groups, taps)[sub:, :].astype(o_ref.dtype)
    carry_ref[j] = h[tm - sub:, :]


def _ffn_combine(groups, taps):
    return jax.nn.silu(_conv3_ext(groups[0], taps[0])) * _conv3_ext(groups[1], taps[1])


def _short_conv_combine(groups, taps):
    return groups[0] * _conv3_ext(groups[1] * groups[2], taps[0])


def norm_matmul_conv(x, gain, w, conv_w, seq, n_groups, combine, name):
    m, k = x.shape
    n = w.shape[1] // n_groups
    tm = _pick(seq, (1024, 512, 256, 128))
    tn = _pick(n, (512, 256, 128))
    nj = n // tn
    n_cw = conv_w.shape[1] // n
    est = (2 * _nbytes((tm, k), F32) + _nbytes((tm, k), BF16) + 3 * n_groups * _nbytes((k, tn), BF16)
           + 2 * _nbytes((tm, tn), BF16) + (n_groups + 2) * _nbytes((tm, tn), F32)
           + 8 * n_groups * _nbytes((CONV_ROW_CHUNK, tn), F32) + nj * n_groups * _nbytes((V7X_SUBLANES, tn), F32))
    w_specs = [pl.BlockSpec((k, tn), lambda i, j, g=g: (0, g * nj + j)) for g in range(n_groups)]
    cw_specs = [pl.BlockSpec((CONV_WIDTH, tn), lambda i, j, g=g: (0, g * nj + j)) for g in range(n_cw)]
    return pl.pallas_call(
        functools.partial(_norm_matmul_conv_body, n_groups=n_groups, n_taps=n_cw, tiles_per_seq=seq // tm,
                          combine=combine),
        grid=(m // tm, nj),
        in_specs=[pl.BlockSpec((tm, k), lambda i, j: (i, 0)), pl.BlockSpec((1, k), lambda i, j: (0, 0))]
        + w_specs + cw_specs,
        out_specs=pl.BlockSpec((tm, tn), lambda i, j: (i, j)),
        out_shape=jax.ShapeDtypeStruct((m, n), BF16),
        scratch_shapes=[pltpu.VMEM((tm, k), BF16),
                        pltpu.VMEM((nj, V7X_SUBLANES, n_groups * tn), F32)],
        compiler_params=pltpu.CompilerParams(
            dimension_semantics=("arbitrary", "arbitrary"), vmem_limit_bytes=_vmem_limit(est)),
        name=name,
    )(x, gain.reshape(1, k), *([w] * n_groups), *([conv_w] * n_cw))


def _matmul_res_body(a_ref, w_ref, r_ref, o_ref):
    o_ref[...] = r_ref[...] + jnp.dot(a_ref[...], w_ref[...], preferred_element_type=F32)


def matmul_residual(a, w, res, name):
    m, k = a.shape
    n = w.shape[1]
    tm = _pick(m, (1024, 512, 256, 128))
    tn = _pick(n, (512, 256, 128)) if k > 4096 else _pick(n, (1024, 512, 256, 128))
    est = (2 * _nbytes((tm, k), BF16) + 2 * _nbytes((k, tn), BF16) + 6 * _nbytes((tm, tn), F32))
    return pl.pallas_call(
        _matmul_res_body,
        grid=(m // tm, n // tn),
        in_specs=[
            pl.BlockSpec((tm, k), lambda i, j: (i, 0)),
            pl.BlockSpec((k, tn), lambda i, j: (0, j)),
            pl.BlockSpec((tm, tn), lambda i, j: (i, j)),
        ],
        out_specs=pl.BlockSpec((tm, tn), lambda i, j: (i, j)),
        out_shape=jax.ShapeDtypeStruct((m, n), F32),
        compiler_params=pltpu.CompilerParams(
            dimension_semantics=("parallel", "parallel"), vmem_limit_bytes=_vmem_limit(est)),
        name=name,
    )(a, w, res)


def _mix_out_body(ylo_ref, yhi_ref, yb_ref, wa_ref, wb_ref, r_ref, o_ref):
    half = ylo_ref.shape[0]
    for rows, ya_ref in ((slice(0, half), ylo_ref), (slice(half, 2 * half), yhi_ref)):
        acc = jnp.dot(ya_ref[...], wa_ref[...], preferred_element_type=F32)
        acc = acc + jnp.dot(yb_ref[rows, :], wb_ref[...], preferred_element_type=F32)
        o_ref[rows, :] = r_ref[rows, :] + acc


def mix_out_proj(ya_lo, ya_hi, yb, w_out, res, seq, name):
    m, kb = yb.shape
    ka = ya_lo.shape[1]
    n = w_out.shape[1]
    half = seq // 2
    tn = _pick(n, (512, 256, 128))
    est = (4 * _nbytes((half, ka), BF16) + 2 * _nbytes((seq, kb), BF16) + 2 * _nbytes((ka + kb, tn), BF16)
           + 5 * _nbytes((seq, tn), F32))
    return pl.pallas_call(
        _mix_out_body,
        grid=(m // seq, n // tn),
        in_specs=[
            pl.BlockSpec((half, ka), lambda i, j: (i, 0)),
            pl.BlockSpec((half, ka), lambda i, j: (i, 0)),
            pl.BlockSpec((seq, kb), lambda i, j: (i, 0)),
            pl.BlockSpec((ka, tn), lambda i, j: (0, j)),
            pl.BlockSpec((kb, tn), lambda i, j: (0, j)),
            pl.BlockSpec((seq, tn), lambda i, j: (i, j)),
        ],
        out_specs=pl.BlockSpec((seq, tn), lambda i, j: (i, j)),
        out_shape=jax.ShapeDtypeStruct((m, n), F32),
        compiler_params=pltpu.CompilerParams(
            dimension_semantics=("parallel", "parallel"), vmem_limit_bytes=_vmem_limit(est)),
        name=name,
    )(ya_lo, ya_hi, yb, w_out[:ka], w_out[ka:], res)


def _head_rms(x, gain):
    return x * lax.rsqrt(jnp.mean(x * x, axis=-1, keepdims=True) + RMS_EPS) * gain


def _moba_query_block(ii, q, q_gain, slope, rel_t, kn_ref, vt_ref, km_ref):
    bs = MOBA_BLOCK
    hd = A_HEAD_DIM
    qn = _head_rms(q, q_gain)
    qs = (qn * (hd ** -0.5 * LOG2E)).astype(BF16)
    slope2 = slope * LOG2E
    bias_rel = slope2 * rel_t
    gt = lax.dot_general(km_ref[...], qn, (((1,), (1,)), ((), ())),
                         precision=lax.Precision.HIGHEST, preferred_element_type=F32)
    blk = lax.broadcasted_iota(jnp.int32, gt.shape, 0)
    gm = jnp.where(blk < ii, gt, NEG_INF)
    scores = []
    m_all = None
    for n in range(ii + 1):
        s = lax.dot_general(kn_ref[n * bs:(n + 1) * bs, :], qs, (((1,), (1,)), ((), ())),
                            preferred_element_type=F32) - bias_rel
        if n == ii:
            s = jnp.where(rel_t >= 0, s, NEG_INF)
            shift = jnp.zeros((1, bs), F32)
        else:
            g_n = gm[n:n + 1, :]
            beats = jnp.where((gm > g_n) | ((gm == g_n) & (blk < n)), 1.0, 0.0)
            chosen = jnp.sum(beats, axis=0, keepdims=True) < float(MOBA_TOPK)
            shift = jnp.where(chosen, slope2 * float((ii - n) * bs), -NEG_INF)
        m_blk = jnp.max(s, axis=0, keepdims=True) - shift
        m_all = m_blk if m_all is None else jnp.maximum(m_all, m_blk)
        scores.append((s, shift))
    acc = None
    for n, (s, shift) in enumerate(scores):
        p = jnp.exp2(s - (m_all + shift)).astype(BF16)
        pv = jnp.dot(vt_ref[:, n * bs:(n + 1) * bs], p, preferred_element_type=F32)
        acc = pv if acc is None else acc + pv
    return (acc[:hd] / acc[hd:hd + 1]).T


def _moba_body(qlo_ref, qhi_ref, k_ref, v_ref, qg_ref, kg_ref, olo_ref, ohi_ref, kn_ref, vt_ref, km_ref,
               *, n_heads, n_blocks):
    bs = MOBA_BLOCK
    hd = A_HEAD_DIM
    hp = pl.program_id(1)
    j = pl.program_id(2)

    @pl.when(j == 0)
    def _():
        km_ref[...] = jnp.zeros_like(km_ref)
        for e in range(MOBA_HEADS_PER_STEP):
            cols = slice(e * hd, (e + 1) * hd)
            for nb in range(n_blocks):
                rows = slice(nb * bs, (nb + 1) * bs)
                kn = _head_rms(k_ref[0, rows, cols], kg_ref[...])
                kn_ref[e, rows, :] = kn.astype(BF16)
                km_ref[e, nb:nb + 1, :] = jnp.mean(kn, axis=0, keepdims=True)
                vt_ref[e, :hd, rows] = v_ref[0, rows, cols].T.astype(BF16)
            vt_ref[e, hd:, :] = jnp.ones((MOBA_DENOM_ROWS, vt_ref.shape[2]), BF16)

    rel_t = (lax.broadcasted_iota(jnp.int32, (bs, bs), 1)
             - lax.broadcasted_iota(jnp.int32, (bs, bs), 0)).astype(F32)

    for jj in range(n_blocks // 2):
        @pl.when(j == jj)
        def _(jj=jj):
            for e in range(MOBA_HEADS_PER_STEP):
                cols = slice(e * hd, (e + 1) * hd)
                head = (hp * MOBA_HEADS_PER_STEP + e + 1).astype(F32)
                slope = jnp.exp2(jnp.full((1, bs), -8.0 / n_heads, F32) * head)
                for q_ref, o_ref, ii in ((qlo_ref, olo_ref, jj), (qhi_ref, ohi_ref, n_blocks - 1 - jj)):
                    o = _moba_query_block(ii, q_ref[0, :, cols], qg_ref[...], slope, rel_t,
                                          kn_ref.at[e], vt_ref.at[e], km_ref.at[e])
                    o_ref[0, :, cols] = o.astype(o_ref.dtype)


def moba_attention(proj, q_gain, k_gain, name):
    b, s, _ = proj.shape
    assert s % MOBA_BLOCK == 0
    nb = s // MOBA_BLOCK
    assert nb % 2 == 0 and A_HEADS % MOBA_HEADS_PER_STEP == 0
    nbp = -(-nb // V7X_SUBLANES) * V7X_SUBLANES
    hd = A_HEAD_DIM
    hw = MOBA_HEADS_PER_STEP * hd
    groups = A_HEADS // MOBA_HEADS_PER_STEP
    half = nb // 2
    est = (8 * _nbytes((MOBA_BLOCK, hw), F32) + 4 * _nbytes((s, hw), F32) + 2 * _nbytes((s, hw), BF16)
           + MOBA_HEADS_PER_STEP * (4 * nb + 16) * _nbytes((MOBA_BLOCK, MOBA_BLOCK), F32))
    o_lo, o_hi = pl.pallas_call(
        functools.partial(_moba_body, n_heads=A_HEADS, n_blocks=nb),
        grid=(b, groups, half),
        in_specs=[
            pl.BlockSpec((1, MOBA_BLOCK, hw), lambda bi, g, j: (bi, j, g)),
            pl.BlockSpec((1, MOBA_BLOCK, hw), lambda bi, g, j: (bi, nb - 1 - j, g)),
            pl.BlockSpec((1, s, hw), lambda bi, g, j: (bi, 0, groups + g)),
            pl.BlockSpec((1, s, hw), lambda bi, g, j: (bi, 0, 2 * groups + g)),
            pl.BlockSpec((1, hd), lambda bi, g, j: (0, 0)),
            pl.BlockSpec((1, hd), lambda bi, g, j: (0, 0)),
        ],
        out_specs=[
            pl.BlockSpec((1, MOBA_BLOCK, hw), lambda bi, g, j: (bi, j, g)),
            pl.BlockSpec((1, MOBA_BLOCK, hw), lambda bi, g, j: (bi, half - 1 - j, g)),
        ],
        out_shape=[jax.ShapeDtypeStruct((b, s // 2, A_WIDTH), BF16)] * 2,
        scratch_shapes=[
            pltpu.VMEM((MOBA_HEADS_PER_STEP, s, hd), BF16),
            pltpu.VMEM((MOBA_HEADS_PER_STEP, hd + MOBA_DENOM_ROWS, s), BF16),
            pltpu.VMEM((MOBA_HEADS_PER_STEP, nbp, hd), F32),
        ],
        compiler_params=pltpu.CompilerParams(
            dimension_semantics=("parallel", "parallel", "arbitrary"), vmem_limit_bytes=_vmem_limit(est)),
        name=name,
    )(proj, proj, proj, proj, q_gain.reshape(1, hd), k_gain.reshape(1, hd))
    return o_lo, o_hi


def _token_shift(p, prev_last, mu):
    rolled = pltpu.roll(p, 1, 0)
    row = lax.broadcasted_iota(jnp.int32, p.shape, 0)
    p_prev = jnp.where(row == 0, prev_last, rolled)
    return p + mu * (p_prev - p)


def _head_sums(x, ones_ref):
    tiles = [jnp.dot(x[:, c:c + V7X_LANES].astype(BF16), ones_ref[...], preferred_element_type=F32)
             for c in range(0, x.shape[1], V7X_LANES)]
    return jnp.concatenate(tiles, axis=1)


def _rwkv_prep_body(*refs, has_vres):
    if has_vres:
        (pm_ref, pz_ref, mum_ref, muz_ref, w0_ref, wl_ref, a0_ref, al_ref, gl_ref, kk_ref, ka_ref, rk_ref,
         ones_ref, v0_ref, vl_ref, vf_ref,
         r_ref, ld_ref, k_ref, v_ref, kn_ref, b_ref, bonus_ref, g_ref, cm_ref, cz_ref) = refs
    else:
        (pm_ref, pz_ref, mum_ref, muz_ref, w0_ref, wl_ref, a0_ref, al_ref, gl_ref, kk_ref, ka_ref, rk_ref,
         ones_ref,
         r_ref, ld_ref, k_ref, v_ref, kn_ref, b_ref, bonus_ref, g_ref, cm_ref, cz_ref) = refs
    tm = pm_ref.shape[1]
    bw = B_WIDTH

    @pl.when(pl.program_id(1) == 0)
    def _():
        cm_ref[...] = jnp.zeros_like(cm_ref)
        cz_ref[...] = jnp.zeros_like(cz_ref)

    pm = pm_ref[0].astype(F32)
    pz = pz_ref[0]
    last = V7X_SUBLANES - 1
    pms = _token_shift(pm, cm_ref[last:last + 1, :], mum_ref[...])
    pzs = _token_shift(pz, cz_ref[last:last + 1, :], muz_ref[...])
    cm_ref[...] = pm[tm - V7X_SUBLANES:, :]
    cz_ref[...] = pz[tm - V7X_SUBLANES:, :]

    r = pms[:, :bw]
    k = pms[:, bw:2 * bw]
    v = pms[:, 2 * bw:3 * bw]

    z_w = jnp.tanh(pzs[:, ZW_OFF:ZW_OFF + ZW_PAD]).astype(BF16)
    w = w0_ref[...] + jnp.dot(z_w, wl_ref[...], preferred_element_type=F32)
    w = -jax.nn.softplus(-w) - 0.5
    ld_ref[0] = -jnp.exp(w)

    z_a = pzs[:, ZA_OFF:ZA_OFF + ZA_PAD].astype(BF16)
    a = jax.nn.sigmoid(a0_ref[...] + jnp.dot(z_a, al_ref[...], preferred_element_type=F32))

    z_g = jax.nn.sigmoid(pzs[:, ZG_OFF:ZG_OFF + ZG_PAD]).astype(BF16)
    g_ref[0] = jnp.dot(z_g, gl_ref[...], preferred_element_type=F32).astype(g_ref.dtype)

    if has_vres:
        z_v = pzs[:, ZV_OFF:ZV_OFF + ZV_PAD].astype(BF16)
        mix = jax.nn.sigmoid(v0_ref[...] + jnp.dot(z_v, vl_ref[...], preferred_element_type=F32))
        v = v + (vf_ref[0].astype(F32) - v) * mix

    kk = k * kk_ref[...]
    kk = kk * jnp.minimum(lax.rsqrt(_head_sums(kk * kk, ones_ref)), 1e12)
    k = k * (1.0 + (a - 1.0) * ka_ref[...])
    r_ref[0] = r.astype(r_ref.dtype)
    k_ref[0] = k.astype(k_ref.dtype)
    v_ref[0] = v.astype(v_ref.dtype)
    kn_ref[0] = kk.astype(kn_ref.dtype)
    b_ref[0] = (kk * a).astype(b_ref.dtype)
    bonus_ref[0] = (_head_sums(r * k * rk_ref[...], ones_ref) * v).astype(bonus_ref.dtype)


def _pad_rows(w, rows):
    return jnp.pad(w, ((0, rows - w.shape[0]), (0, 0)))


def _head_pair_ones():
    head = jnp.arange(V7X_LANES) // B_HEAD_DIM
    return (head[:, None] == head[None, :]).astype(BF16)


def rwkv_prep(proj, pz, shift_mu, w0, w_lora, a0, a_lora, g_lora, k_k, k_a, r_k, v0, v_lora, v_first, name):
    b, s, main_blk = proj.shape
    bw = B_WIDTH
    has_vres = v_lora is not None
    tm = _pick(s, (256, 128))
    assert main_blk == 3 * bw

    mu_main = shift_mu[:3 * bw].reshape(1, 3 * bw)
    lora_mu = shift_mu[3 * bw:]
    mu_z = jnp.zeros((Z_WIDTH,), F32)
    mu_z = mu_z.at[ZW_OFF:ZW_OFF + LORA_W].set(lora_mu[:LORA_W])
    mu_z = mu_z.at[ZA_OFF:ZA_OFF + LORA_A].set(lora_mu[LORA_W:LORA_W + LORA_A])
    mu_z = mu_z.at[ZG_OFF:ZG_OFF + LORA_G].set(lora_mu[LORA_W + LORA_A:LORA_W + LORA_A + LORA_G])
    if has_vres:
        mu_z = mu_z.at[ZV_OFF:ZV_OFF + LORA_V].set(lora_mu[LORA_W + LORA_A + LORA_G:])
    mu_z = mu_z.reshape(1, Z_WIDTH)

    row = lambda t: t.reshape(1, bw)
    const = lambda shape: pl.BlockSpec(shape, lambda bi, i: (0, 0))
    tok = pl.BlockSpec((1, tm, bw), lambda bi, i: (bi, i, 0))
    args = [proj, pz, mu_main, mu_z, row(w0), _pad_rows(w_lora, ZW_PAD).astype(BF16),
            row(a0), _pad_rows(a_lora, ZA_PAD).astype(BF16), _pad_rows(g_lora, ZG_PAD).astype(BF16),
            row(k_k), row(k_a), row(r_k), _head_pair_ones()]
    in_specs = [
        pl.BlockSpec((1, tm, main_blk), lambda bi, i: (bi, i, 0)),
        pl.BlockSpec((1, tm, Z_WIDTH), lambda bi, i: (bi, i, 0)),
        const((1, main_blk)), const((1, Z_WIDTH)),
        const((1, bw)), const((ZW_PAD, bw)), const((1, bw)), const((ZA_PAD, bw)), const((ZG_PAD, bw)),
        const((1, bw)), const((1, bw)), const((1, bw)), const((V7X_LANES, V7X_LANES)),
    ]
    if has_vres:
        args += [row(v0), _pad_rows(v_lora, ZV_PAD).astype(BF16), v_first]
        in_specs += [const((1, bw)), const((ZV_PAD, bw)), tok]
    out_dtypes = [BF16, F32, BF16, BF16, BF16, BF16, BF16, BF16]
    n_out = len(out_dtypes)
    est = (2 * _nbytes((tm, main_blk + Z_WIDTH), F32) + (2 * n_out + 6) * _nbytes((tm, bw), F32)
           + 6 * _nbytes((tm, main_blk), F32) + 4 * _nbytes((ZG_PAD + 3 * ZW_PAD, bw), BF16))
    return pl.pallas_call(
        functools.partial(_rwkv_prep_body, has_vres=has_vres),
        grid=(b, s // tm),
        in_specs=in_specs,
        out_specs=[tok] * n_out,
        out_shape=[jax.ShapeDtypeStruct((b, s, bw), dt) for dt in out_dtypes],
        scratch_shapes=[pltpu.VMEM((V7X_SUBLANES, main_blk), F32), pltpu.VMEM((V7X_SUBLANES, Z_WIDTH), F32)],
        compiler_params=pltpu.CompilerParams(
            dimension_semantics=("parallel", "arbitrary"), vmem_limit_bytes=_vmem_limit(est)),
        name=name,
    )(*args)


def _cumsum_rows(x):
    row = lax.broadcasted_iota(jnp.int32, x.shape, 0)
    sh = 1
    while sh < x.shape[0]:
        x = x + jnp.where(row >= sh, pltpu.roll(x, sh, 0), 0.0)
        sh *= 2
    return x


def _stack_heads(x, m0, m1):
    return jnp.concatenate([x * m0, x * m1], axis=0)


def _nt_dot(a, b):
    return lax.dot_general(a.astype(BF16), b.astype(BF16), (((1,), (1,)), ((), ())), preferred_element_type=F32)


def _tn_dot(a, b):
    return lax.dot_general(a.astype(BF16), b.astype(BF16), (((0,), (0,)), ((), ())), preferred_element_type=F32)


def _nn_dot(a, b):
    return jnp.dot(a.astype(BF16), b.astype(BF16), preferred_element_type=F32)


def _rwkv_chunk(streams, hts, masks):
    pairs = range(len(streams))
    c = streams[0][0].shape[0]
    n2 = 2 * c
    m0, m1, strict, incl, eye, levels = masks
    ar, bk, v_st, p_last = [], [], [], []
    for r, ld, k, v, kk, b in streams:
        cs = _cumsum_rows(ld)
        p = jnp.exp(cs)
        p_inv = jnp.exp(-cs)
        a_st = _stack_heads(-kk * jnp.exp(cs - ld), m0, m1)
        r_st = _stack_heads(r * p, m0, m1)
        b_st = _stack_heads(b * p_inv, m0, m1)
        k_st = _stack_heads(k * p_inv, m0, m1)
        ar.append(jnp.concatenate([a_st, r_st], axis=0).astype(BF16))
        bk.append(jnp.concatenate([b_st, k_st], axis=0).astype(BF16))
        v_st.append(_stack_heads(v, m0, m1))
        p_last.append(p[c - 1:c, :])

    q = [_nt_dot(ar[i], bk[i]) for i in pairs]
    l_ab = [q[i][:n2, :n2] * strict for i in pairs]
    t = [eye + l_ab[i] * levels[0] for i in pairs]
    for lvl in levels[1:]:
        tc = [_nn_dot(t[i], l_ab[i] * lvl) for i in pairs]
        t = [t[i] + _nn_dot(tc[i], t[i]) for i in pairs]

    ar_h = [_nt_dot(ar[i], hts[i]) for i in pairs]
    ak_v = [_nn_dot(q[i][:n2, n2:] * strict, v_st[i]) for i in pairs]
    u_st = [_nn_dot(t[i], ar_h[i][:n2] + ak_v[i]) for i in pairs]
    uv = [jnp.concatenate([u_st[i], v_st[i]], axis=0).astype(BF16) for i in pairs]
    y_st = [ar_h[i][n2:] + _nn_dot(q[i][n2:, :] * jnp.concatenate([incl, incl], axis=1), uv[i]) for i in pairs]
    ys = [y_st[i][:c] + y_st[i][c:] for i in pairs]
    ht_new = [(hts[i] + _tn_dot(uv[i], bk[i])) * p_last[i] for i in pairs]
    return ys, ht_new


def _rwkv_scan_body(r_ref, ld_ref, k_ref, v_ref, kk_ref, b_ref, bonus_ref, g_ref, gw_ref, gb_ref, ones_ref,
                    o_ref, ht_ref):
    c = r_ref.shape[1]
    n2 = 2 * c
    lanes = V7X_LANES

    @pl.when(pl.program_id(1) == 0)
    def _():
        ht_ref[...] = jnp.zeros_like(ht_ref)

    lane = lax.broadcasted_iota(jnp.int32, (1, lanes), 1)
    m0 = jnp.where(lane < B_HEAD_DIM, 1.0, 0.0)
    m1 = 1.0 - m0
    row = lax.broadcasted_iota(jnp.int32, (n2, n2), 0)
    col = lax.broadcasted_iota(jnp.int32, (n2, n2), 1)
    log2c = c.bit_length() - 1
    same = jnp.right_shift(row, log2c) == jnp.right_shift(col, log2c)
    strict = jnp.where(same & (row > col), 1.0, 0.0)
    incl = jnp.where(same & (row >= col), 1.0, 0.0)
    eye = jnp.where(row == col, 1.0, 0.0)
    levels = []
    for lh in range(log2c):
        blk = jnp.right_shift(row, lh + 1) == jnp.right_shift(col, lh + 1)
        lower_left = (jnp.bitwise_and(jnp.right_shift(row, lh), 1) == 1) & \
                     (jnp.bitwise_and(jnp.right_shift(col, lh), 1) == 0)
        levels.append(jnp.where(blk & lower_left, 1.0, 0.0))
    masks = (m0, m1, strict, incl, eye, levels)

    where = [(bi, slice(i * lanes, (i + 1) * lanes))
             for bi in range(r_ref.shape[0]) for i in range(r_ref.shape[2] // lanes)]
    streams = [tuple(ref[bi, :, cs].astype(F32) for ref in (r_ref, ld_ref, k_ref, v_ref, kk_ref, b_ref))
               for bi, cs in where]
    ys, ht_new = _rwkv_chunk(streams, [ht_ref[i] for i in range(len(where))], masks)
    for i in range(len(where)):
        ht_ref[i] = ht_new[i]
    inv_n = 1.0 / B_HEAD_DIM
    means = [jnp.dot(y.astype(BF16), ones_ref[...], preferred_element_type=F32) * inv_n for y in ys]
    ds = [y - mean for y, mean in zip(ys, means)]
    vrs = [jnp.dot((d * d).astype(BF16), ones_ref[...], preferred_element_type=F32) * inv_n for d in ds]
    for (bi, cs), d, var in zip(where, ds, vrs):
        yn = d * lax.rsqrt(var + GN_EPS) * gw_ref[:, cs] + gb_ref[:, cs]
        o_ref[bi, :, cs] = ((yn + bonus_ref[bi, :, cs].astype(F32)) * g_ref[bi, :, cs].astype(F32)
                            ).astype(o_ref.dtype)


def rwkv_scan(r, ld, k, v, kk, b, bonus, g, gn_w, gn_b, name):
    bsz, s, bw = r.shape
    c = RWKV_CHUNK
    nb = _pick(bsz, (RWKV_SEQS_PER_STEP, 1))
    assert s % c == 0 and c & (c - 1) == 0 and 2 * B_HEAD_DIM == V7X_LANES
    tok = pl.BlockSpec((nb, c, bw), lambda bi, i: (bi, i, 0))
    par = pl.BlockSpec((1, bw), lambda bi, i: (0, 0))
    chains = nb * (bw // V7X_LANES)
    est = (18 * nb * _nbytes((c, bw), F32) + chains * _nbytes((V7X_LANES, V7X_LANES), F32)
           + 8 * chains * _nbytes((4 * c, 4 * c), F32))
    return pl.pallas_call(
        _rwkv_scan_body,
        grid=(bsz // nb, s // c),
        in_specs=[tok] * 8 + [par] * 2 + [pl.BlockSpec((V7X_LANES, V7X_LANES), lambda bi, i: (0, 0))],
        out_specs=tok,
        out_shape=jax.ShapeDtypeStruct((bsz, s, bw), BF16),
        scratch_shapes=[pltpu.VMEM((chains, V7X_LANES, V7X_LANES), F32)],
        compiler_params=pltpu.CompilerParams(
            dimension_semantics=("parallel", "arbitrary"), vmem_limit_bytes=_vmem_limit(est)),
        name=name,
    )(r, ld, k, v, kk, b, bonus, g, gn_w.reshape(1, bw), gn_b.reshape(1, bw), _head_pair_ones())


def _lora_proj_weight(w_in, has_vres):
    o = 3 * A_WIDTH + 3 * B_WIDTH
    d = w_in.shape[0]
    wz = jnp.zeros((d, Z_WIDTH), w_in.dtype)
    wz = wz.at[:, ZW_OFF:ZW_OFF + LORA_W].set(w_in[:, o:o + LORA_W])
    o += LORA_W
    wz = wz.at[:, ZA_OFF:ZA_OFF + LORA_A].set(w_in[:, o:o + LORA_A])
    o += LORA_A
    wz = wz.at[:, ZG_OFF:ZG_OFF + LORA_G].set(w_in[:, o:o + LORA_G])
    o += LORA_G
    if has_vres:
        wz = wz.at[:, ZV_OFF:ZV_OFF + LORA_V].set(w_in[:, o:o + LORA_V])
    return wz


def moba_rwkv_layer(x, b, s, li, norm_mix, w_in, q_gain, k_gain, shift_mu, w0, w_lora, a0, a_lora, g_lora,
                    k_k, k_a, r_k, gn_w, gn_b, w_out, v0, v_lora, v_first):
    has_vres = v_lora is not None
    tag = "l%d_" % li
    n_main = 3 * A_WIDTH + 3 * B_WIDTH
    proj_a, proj_b, pz = norm_matmul_split(
        x, norm_mix, w_in.astype(BF16), n_main, 3 * A_WIDTH, _lora_proj_weight(w_in, has_vres).astype(BF16),
        F32, BF16, F32, tag + "in_proj")
    pz = pz.reshape(b, s, Z_WIDTH)

    ya_lo, ya_hi = moba_attention(proj_a.reshape(b, s, 3 * A_WIDTH), q_gain, k_gain, tag + "moba")

    r, ld, k, v, kk, kb, bonus, g = rwkv_prep(proj_b.reshape(b, s, 3 * B_WIDTH), pz, shift_mu, w0, w_lora, a0,
                                              a_lora, g_lora, k_k, k_a, r_k.reshape(-1), v0, v_lora, v_first,
                                              tag + "rwkv_prep")
    if not has_vres:
        v_first = v
    y_b = rwkv_scan(r, ld, k, v, kk, kb, bonus, g, gn_w, gn_b, tag + "rwkv_scan")

    x = mix_out_proj(ya_lo.reshape(b * s // 2, A_WIDTH), ya_hi.reshape(b * s // 2, A_WIDTH),
                     y_b.reshape(b * s, B_WIDTH), w_out.astype(BF16), x, s, tag + "out_proj")
    return x, v_first


def short_conv_layer(x, b, s, li, norm_mix, conv_in, conv_w, conv_out):
    tag = "l%d_" % li
    m = norm_matmul_conv(x, norm_mix, conv_in.astype(BF16), conv_w, s, 3, _short_conv_combine, tag + "conv_in")
    return matmul_residual(m, conv_out.astype(BF16), x, tag + "conv_out")


def conv_ffn(x, b, s, li, norm_ffn, ffn_up, ffn_conv, ffn_down):
    tag = "l%d_" % li
    m = norm_matmul_conv(x, norm_ffn, ffn_up.astype(BF16), ffn_conv, s, 2, _ffn_combine, tag + "ffn_up")
    return matmul_residual(m, ffn_down.astype(BF16), x, tag + "ffn_down")


def kernel(x, l0_norm_mix, l0_w_in, l0_q_gain, l0_k_gain, l0_shift_mu, l0_w0, l0_w_lora, l0_a0, l0_a_lora, l0_g_lora, l0_k_k, l0_k_a, l0_r_k, l0_gn_w, l0_gn_b, l0_w_out, l0_norm_ffn, l0_ffn_up, l0_ffn_conv, l0_ffn_down, l1_norm_mix, l1_conv_in, l1_conv_w, l1_conv_out, l1_norm_ffn, l1_ffn_up, l1_ffn_conv, l1_ffn_down, l2_norm_mix, l2_w_in, l2_q_gain, l2_k_gain, l2_shift_mu, l2_w0, l2_w_lora, l2_a0, l2_a_lora, l2_g_lora, l2_k_k, l2_k_a, l2_r_k, l2_gn_w, l2_gn_b, l2_w_out, l2_v0, l2_v_lora, l2_norm_ffn, l2_ffn_up, l2_ffn_conv, l2_ffn_down, l3_norm_mix, l3_conv_in, l3_conv_w, l3_conv_out, l3_norm_ffn, l3_ffn_up, l3_ffn_conv, l3_ffn_down):
    b, s, d = x.shape
    h = x.reshape(b * s, d)

    h, v_first = moba_rwkv_layer(
        h, b, s, 0, l0_norm_mix, l0_w_in, l0_q_gain, l0_k_gain, l0_shift_mu, l0_w0, l0_w_lora, l0_a0,
        l0_a_lora, l0_g_lora, l0_k_k, l0_k_a, l0_r_k, l0_gn_w, l0_gn_b, l0_w_out, None, None, None)
    h = conv_ffn(h, b, s, 0, l0_norm_ffn, l0_ffn_up, l0_ffn_conv, l0_ffn_down)

    h = short_conv_layer(h, b, s, 1, l1_norm_mix, l1_conv_in, l1_conv_w, l1_conv_out)
    h = conv_ffn(h, b, s, 1, l1_norm_ffn, l1_ffn_up, l1_ffn_conv, l1_ffn_down)

    h, _ = moba_rwkv_layer(
        h, b, s, 2, l2_norm_mix, l2_w_in, l2_q_gain, l2_k_gain, l2_shift_mu, l2_w0, l2_w_lora, l2_a0,
        l2_a_lora, l2_g_lora, l2_k_k, l2_k_a, l2_r_k, l2_gn_w, l2_gn_b, l2_w_out, l2_v0, l2_v_lora, v_first)
    h = conv_ffn(h, b, s, 2, l2_norm_ffn, l2_ffn_up, l2_ffn_conv, l2_ffn_down)

    h = short_conv_layer(h, b, s, 3, l3_norm_mix, l3_conv_in, l3_conv_w, l3_conv_out)
    h = conv_ffn(h, b, s, 3, l3_norm_ffn, l3_ffn_up, l3_ffn_conv, l3_ffn_down)
    return h.reshape(b, s, d)
```

```python
import functools

import jax
import jax.numpy as jnp
from jax import lax
from jax.experimental import pallas as pl
from jax.experimental.pallas import tpu as pltpu

F32 = jnp.float32
BF16 = jnp.bfloat16

A_HEADS = 8
A_HEAD_DIM = 128
A_WIDTH = A_HEADS * A_HEAD_DIM
MOBA_BLOCK = 256
MOBA_TOPK = 3
MOBA_HEADS_PER_STEP = 4
MOBA_DENOM_ROWS = 16
B_HEADS = 16
B_HEAD_DIM = 64
B_WIDTH = B_HEADS * B_HEAD_DIM
RWKV_CHUNK = 64
RWKV_SEQS_PER_STEP = 4
LORA_W = 64
LORA_A = 64
LORA_G = 160
LORA_V = 32
CONV_WIDTH = 3
RMS_EPS = 1e-6
GN_EPS = 64e-5
NEG_INF = -1e30
LOG2E = 1.4426950408889634

V7X_LANES = 128
V7X_SUBLANES = 8
V7X_VMEM_BYTES = 64 * 1024 * 1024
V7X_VMEM_CAP = V7X_VMEM_BYTES - 8 * 1024 * 1024
SPILL_ALLOWANCE = 4 * 1024 * 1024
CONV_ROW_CHUNK = 256

ZW_OFF, ZW_PAD = 0, 128
ZA_OFF, ZA_PAD = 128, 128
ZG_OFF, ZG_PAD = 256, 256
ZV_OFF, ZV_PAD = 512, 128
Z_WIDTH = 640


def _vmem_limit(nbytes):
    return int(min(max(nbytes + SPILL_ALLOWANCE, 16 * 1024 * 1024), V7X_VMEM_CAP))


def _nbytes(shape, dtype):
    n = 1
    for s in shape:
        n *= s
    return n * jnp.dtype(dtype).itemsize


def _pick(n, prefs):
    for p in prefs:
        if n % p == 0:
            return p
    return n


def _norm_matmul_split_body(x_ref, g_ref, w_ref, wz_ref, oa_ref, ob_ref, oz_ref, xn_ref, *, na, nt):
    j = pl.program_id(1)

    @pl.when(j == 0)
    def _():
        x = x_ref[...]
        ms = jnp.mean(x * x, axis=-1, keepdims=True)
        xn = (x * lax.rsqrt(ms + RMS_EPS) * g_ref[...]).astype(BF16)
        xn_ref[...] = xn
        oz_ref[...] = jnp.dot(xn, wz_ref[...], preferred_element_type=F32).astype(oz_ref.dtype)

    @pl.when(j > 0)
    def _():
        h = jnp.dot(xn_ref[...], w_ref[...], preferred_element_type=F32)

        @pl.when(j <= na)
        def _():
            oa_ref[...] = h.astype(oa_ref.dtype)

        @pl.when(j > na)
        def _():
            ob_ref[...] = h.astype(ob_ref.dtype)


def norm_matmul_split(x, gain, w, n, n_a, wz, dtype_a, dtype_b, dtype_z, name):
    m, k = x.shape
    nz = wz.shape[1]
    tm = _pick(m, (1024, 512, 256, 128))
    tn = _pick(n_a, (1024, 512, 256, 128))
    assert n % tn == 0
    na = n_a // tn
    nt = n // tn
    est = (2 * _nbytes((tm, k), F32) + _nbytes((tm, k), BF16) + 2 * _nbytes((k, tn + nz), BF16)
           + 8 * _nbytes((tm, tn), F32) + 3 * _nbytes((tm, nz), F32) + _nbytes((tm, k), F32))
    return pl.pallas_call(
        functools.partial(_norm_matmul_split_body, na=na, nt=nt),
        grid=(m // tm, nt + 1),
        in_specs=[
            pl.BlockSpec((tm, k), lambda i, j: (i, 0)),
            pl.BlockSpec((1, k), lambda i, j: (0, 0)),
            pl.BlockSpec((k, tn), lambda i, j: (0, jnp.maximum(j - 1, 0))),
            pl.BlockSpec((k, nz), lambda i, j: (0, 0)),
        ],
        out_specs=[
            pl.BlockSpec((tm, tn), lambda i, j: (i, jnp.clip(j - 1, 0, na - 1))),
            pl.BlockSpec((tm, tn), lambda i, j: (i, jnp.clip(j - 1 - na, 0, nt - na - 1))),
            pl.BlockSpec((tm, nz), lambda i, j: (i, 0)),
        ],
        out_shape=[jax.ShapeDtypeStruct((m, n_a), dtype_a), jax.ShapeDtypeStruct((m, n - n_a), dtype_b),
                   jax.ShapeDtypeStruct((m, nz), dtype_z)],
        scratch_shapes=[pltpu.VMEM((tm, k), BF16)],
        compiler_params=pltpu.CompilerParams(
            dimension_semantics=("arbitrary", "arbitrary"), vmem_limit_bytes=_vmem_limit(est)),
        name=name,
    )(x, gain.reshape(1, k), w, wz)


def _conv3_ext(u, w):
    return w[0:1, :] * pltpu.roll(u, 2, 0) + w[1:2, :] * pltpu.roll(u, 1, 0) + w[2:3, :] * u


def _norm_matmul_conv_body(x_ref, g_ref, *refs, n_groups, n_taps, tiles_per_seq, combine):
    w_refs = refs[:n_groups]
    cw_refs = refs[n_groups:n_groups + n_taps]
    o_ref, xn_ref, carry_ref = refs[n_groups + n_taps:]
    i = pl.program_id(0)
    j = pl.program_id(1)
    tm, tn = o_ref.shape
    sub = V7X_SUBLANES
    rc = min(tm, CONV_ROW_CHUNK)

    @pl.when(j == 0)
    def _():
        x = x_ref[...]
        ms = jnp.mean(x * x, axis=-1, keepdims=True)
        xn_ref[...] = (x * lax.rsqrt(ms + RMS_EPS) * g_ref[...]).astype(BF16)

    @pl.when(i % tiles_per_seq == 0)
    def _():
        carry_ref[j] = jnp.zeros(carry_ref.shape[1:], F32)

    w = jnp.concatenate([w_ref[...] for w_ref in w_refs], axis=1)
    h = jnp.dot(xn_ref[...], w, preferred_element_type=F32)
    taps = [cw_ref[...] for cw_ref in cw_refs]
    for c in range(tm // rc):
        r0 = c * rc
        if c == 0:
            ext = jnp.concatenate([carry_ref[j], h[:rc, :]], axis=0)
        else:
            ext = h[r0 - sub:r0 + rc, :]
        groups = [ext[:, g * tn:(g + 1) * tn] for g in range(n_groups)]
        o_ref[r0:r0 + rc, :] = combine(groups, taps)[sub:, :].astype(o_ref.dtype)
    carry_ref[j] = h[tm - sub:, :]


def _ffn_combine(groups, taps):
    return jax.nn.silu(_conv3_ext(groups[0], taps[0])) * _conv3_ext(groups[1], taps[1])


def _short_conv_combine(groups, taps):
    return groups[0] * _conv3_ext(groups[1] * groups[2], taps[0])


def norm_matmul_conv(x, gain, w, conv_w, seq, n_groups, combine, name):
    m, k = x.shape
    n = w.shape[1] // n_groups
    tm = _pick(seq, (1024, 512, 256, 128))
    tn = _pick(n, (512, 256, 128))
    nj = n // tn
    n_cw = conv_w.shape[1] // n
    est = (2 * _nbytes((tm, k), F32) + _nbytes((tm, k), BF16) + 3 * n_groups * _nbytes((k, tn), BF16)
           + 2 * _nbytes((tm, tn), BF16) + (n_groups + 2) * _nbytes((tm, tn), F32)
           + 8 * n_groups * _nbytes((CONV_ROW_CHUNK, tn), F32) + nj * n_groups * _nbytes((V7X_SUBLANES, tn), F32))
    w_specs = [pl.BlockSpec((k, tn), lambda i, j, g=g: (0, g * nj + j)) for g in range(n_groups)]
    cw_specs = [pl.BlockSpec((CONV_WIDTH, tn), lambda i, j, g=g: (0, g * nj + j)) for g in range(n_cw)]
    return pl.pallas_call(
        functools.partial(_norm_matmul_conv_body, n_groups=n_groups, n_taps=n_cw, tiles_per_seq=seq // tm,
                          combine=combine),
        grid=(m // tm, nj),
        in_specs=[pl.BlockSpec((tm, k), lambda i, j: (i, 0)), pl.BlockSpec((1, k), lambda i, j: (0, 0))]
        + w_specs + cw_specs,
        out_specs=pl.BlockSpec((tm, tn), lambda i, j: (i, j)),
        out_shape=jax.ShapeDtypeStruct((m, n), BF16),
        scratch_shapes=[pltpu.VMEM((tm, k), BF16),
                        pltpu.VMEM((nj, V7X_SUBLANES, n_groups * tn), F32)],
        compiler_params=pltpu.CompilerParams(
            dimension_semantics=("arbitrary", "arbitrary"), vmem_limit_bytes=_vmem_limit(est)),
        name=name,
    )(x, gain.reshape(1, k), *([w] * n_groups), *([conv_w] * n_cw))


def _matmul_res_body(a_ref, w_ref, r_ref, o_ref):
    o_ref[...] = r_ref[...] + jnp.dot(a_ref[...], w_ref[...], preferred_element_type=F32)


def matmul_residual(a, w, res, name):
    m, k = a.shape
    n = w.shape[1]
    tm = _pick(m, (1024, 512, 256, 128))
    tn = _pick(n, (512, 256, 128)) if k > 4096 else _pick(n, (1024, 512, 256, 128))
    est = (2 * _nbytes((tm, k), BF16) + 2 * _nbytes((k, tn), BF16) + 6 * _nbytes((tm, tn), F32))
    return pl.pallas_call(
        _matmul_res_body,
        grid=(m // tm, n // tn),
        in_specs=[
            pl.BlockSpec((tm, k), lambda i, j: (i, 0)),
            pl.BlockSpec((k, tn), lambda i, j: (0, j)),
            pl.BlockSpec((tm, tn), lambda i, j: (i, j)),
        ],
        out_specs=pl.BlockSpec((tm, tn), lambda i, j: (i, j)),
        out_shape=jax.ShapeDtypeStruct((m, n), F32),
        compiler_params=pltpu.CompilerParams(
            dimension_semantics=("parallel", "parallel"), vmem_limit_bytes=_vmem_limit(est)),
        name=name,
    )(a, w, res)


def _mix_out_body(ylo_ref, yhi_ref, yb_ref, wa_ref, wb_ref, r_ref, o_ref):
    half = ylo_ref.shape[0]
    for rows, ya_ref in ((slice(0, half), ylo_ref), (slice(half, 2 * half), yhi_ref)):
        acc = jnp.dot(ya_ref[...], wa_ref[...], preferred_element_type=F32)
        acc = acc + jnp.dot(yb_ref[rows, :], wb_ref[...], preferred_element_type=F32)
        o_ref[rows, :] = r_ref[rows, :] + acc


def mix_out_proj(ya_lo, ya_hi, yb, w_out, res, seq, name):
    m, kb = yb.shape
    ka = ya_lo.shape[1]
    n = w_out.shape[1]
    half = seq // 2
    tn = _pick(n, (512, 256, 128))
    est = (4 * _nbytes((half, ka), BF16) + 2 * _nbytes((seq, kb), BF16) + 2 * _nbytes((ka + kb, tn), BF16)
           + 5 * _nbytes((seq, tn), F32))
    return pl.pallas_call(
        _mix_out_body,
        grid=(m // seq, n // tn),
        in_specs=[
            pl.BlockSpec((half, ka), lambda i, j: (i, 0)),
            pl.BlockSpec((half, ka), lambda i, j: (i, 0)),
            pl.BlockSpec((seq, kb), lambda i, j: (i, 0)),
            pl.BlockSpec((ka, tn), lambda i, j: (0, j)),
            pl.BlockSpec((kb, tn), lambda i, j: (0, j)),
            pl.BlockSpec((seq, tn), lambda i, j: (i, j)),
        ],
        out_specs=pl.BlockSpec((seq, tn), lambda i, j: (i, j)),
        out_shape=jax.ShapeDtypeStruct((m, n), F32),
        compiler_params=pltpu.CompilerParams(
            dimension_semantics=("parallel", "parallel"), vmem_limit_bytes=_vmem_limit(est)),
        name=name,
    )(ya_lo, ya_hi, yb, w_out[:ka], w_out[ka:], res)


def _head_rms(x, gain):
    return x * lax.rsqrt(jnp.mean(x * x, axis=-1, keepdims=True) + RMS_EPS) * gain


def _moba_query_block(ii, q, q_gain, slope, rel_t, kn_ref, vt_ref, km_ref):
    bs = MOBA_BLOCK
    hd = A_HEAD_DIM
    qn = _head_rms(q, q_gain)
    qs = (qn * (hd ** -0.5 * LOG2E)).astype(BF16)
    slope2 = slope * LOG2E
    bias_rel = slope2 * rel_t
    gt = lax.dot_general(km_ref[...], qn, (((1,), (1,)), ((), ())),
                         precision=lax.Precision.HIGHEST, preferred_element_type=F32)
    blk = lax.broadcasted_iota(jnp.int32, gt.shape, 0)
    gm = jnp.where(blk < ii, gt, NEG_INF)
    scores = []
    m_all = None
    for n in range(ii + 1):
        s = lax.dot_general(kn_ref[n * bs:(n + 1) * bs, :], qs, (((1,), (1,)), ((), ())),
                            preferred_element_type=F32) - bias_rel
        if n == ii:
            s = jnp.where(rel_t >= 0, s, NEG_INF)
            shift = jnp.zeros((1, bs), F32)
        else:
            g_n = gm[n:n + 1, :]
            beats = jnp.where((gm > g_n) | ((gm == g_n) & (blk < n)), 1.0, 0.0)
            chosen = jnp.sum(beats, axis=0, keepdims=True) < float(MOBA_TOPK)
            shift = jnp.where(chosen, slope2 * float((ii - n) * bs), -NEG_INF)
        m_blk = jnp.max(s, axis=0, keepdims=True) - shift
        m_all = m_blk if m_all is None else jnp.maximum(m_all, m_blk)
        scores.append((s, shift))
    acc = None
    for n, (s, shift) in enumerate(scores):
        p = jnp.exp2(s - (m_all + shift)).astype(BF16)
        pv = jnp.dot(vt_ref[:, n * bs:(n + 1) * bs], p, preferred_element_type=F32)
        acc = pv if acc is None else acc + pv
    return (acc[:hd] / acc[hd:hd + 1]).T


def _moba_body(qlo_ref, qhi_ref, k_ref, v_ref, qg_ref, kg_ref, olo_ref, ohi_ref, kn_ref, vt_ref, km_ref,
               *, n_heads, n_blocks):
    bs = MOBA_BLOCK
    hd = A_HEAD_DIM
    hp = pl.program_id(1)
    j = pl.program_id(2)

    @pl.when(j == 0)
    def _():
        km_ref[...] = jnp.zeros_like(km_ref)
        for e in range(MOBA_HEADS_PER_STEP):
            cols = slice(e * hd, (e + 1) * hd)
            for nb in range(n_blocks):
                rows = slice(nb * bs, (nb + 1) * bs)
                kn = _head_rms(k_ref[0, rows, cols], kg_ref[...])
                kn_ref[e, rows, :] = kn.astype(BF16)
                km_ref[e, nb:nb + 1, :] = jnp.mean(kn, axis=0, keepdims=True)
                vt_ref[e, :hd, rows] = v_ref[0, rows, cols].T.astype(BF16)
            vt_ref[e, hd:, :] = jnp.ones((MOBA_DENOM_ROWS, vt_ref.shape[2]), BF16)

    rel_t = (lax.broadcasted_iota(jnp.int32, (bs, bs), 1)
             - lax.broadcasted_iota(jnp.int32, (bs, bs), 0)).astype(F32)

    for jj in range(n_blocks // 2):
        @pl.when(j == jj)
        def _(jj=jj):
            for e in range(MOBA_HEADS_PER_STEP):
                cols = slice(e * hd, (e + 1) * hd)
                head = (hp * MOBA_HEADS_PER_STEP + e + 1).astype(F32)
                slope = jnp.exp2(jnp.full((1, bs), -8.0 / n_heads, F32) * head)
                for q_ref, o_ref, ii in ((qlo_ref, olo_ref, jj), (qhi_ref, ohi_ref, n_blocks - 1 - jj)):
                    o = _moba_query_block(ii, q_ref[0, :, cols], qg_ref[...], slope, rel_t,
                                          kn_ref.at[e], vt_ref.at[e], km_ref.at[e])
                    o_ref[0, :, cols] = o.astype(o_ref.dtype)


def moba_attention(proj, q_gain, k_gain, name):
    b, s, _ = proj.shape
    assert s % MOBA_BLOCK == 0
    nb = s // MOBA_BLOCK
    assert nb % 2 == 0 and A_HEADS % MOBA_HEADS_PER_STEP == 0
    nbp = -(-nb // V7X_SUBLANES) * V7X_SUBLANES
    hd = A_HEAD_DIM
    hw = MOBA_HEADS_PER_STEP * hd
    groups = A_HEADS // MOBA_HEADS_PER_STEP
    half = nb // 2
    est = (8 * _nbytes((MOBA_BLOCK, hw), F32) + 4 * _nbytes((s, hw), F32) + 2 * _nbytes((s, hw), BF16)
           + MOBA_HEADS_PER_STEP * (4 * nb + 16) * _nbytes((MOBA_BLOCK, MOBA_BLOCK), F32))
    o_lo, o_hi = pl.pallas_call(
        functools.partial(_moba_body, n_heads=A_HEADS, n_blocks=nb),
        grid=(b, groups, half),
        in_specs=[
            pl.BlockSpec((1, MOBA_BLOCK, hw), lambda bi, g, j: (bi, j, g)),
            pl.BlockSpec((1, MOBA_BLOCK, hw), lambda bi, g, j: (bi, nb - 1 - j, g)),
            pl.BlockSpec((1, s, hw), lambda bi, g, j: (bi, 0, groups + g)),
            pl.BlockSpec((1, s, hw), lambda bi, g, j: (bi, 0, 2 * groups + g)),
            pl.BlockSpec((1, hd), lambda bi, g, j: (0, 0)),
            pl.BlockSpec((1, hd), lambda bi, g, j: (0, 0)),
        ],
        out_specs=[
            pl.BlockSpec((1, MOBA_BLOCK, hw), lambda bi, g, j: (bi, j, g)),
            pl.BlockSpec((1, MOBA_BLOCK, hw), lambda bi, g, j: (bi, half - 1 - j, g)),
        ],
        out_shape=[jax.ShapeDtypeStruct((b, s // 2, A_WIDTH), BF16)] * 2,
        scratch_shapes=[
            pltpu.VMEM((MOBA_HEADS_PER_STEP, s, hd), BF16),
            pltpu.VMEM((MOBA_HEADS_PER_STEP, hd + MOBA_DENOM_ROWS, s), BF16),
            pltpu.VMEM((MOBA_HEADS_PER_STEP, nbp, hd), F32),
        ],
        compiler_params=pltpu.CompilerParams(
            dimension_semantics=("parallel", "parallel", "arbitrary"), vmem_limit_bytes=_vmem_limit(est)),
        name=name,
    )(proj, proj, proj, proj, q_gain.reshape(1, hd), k_gain.reshape(1, hd))
    return o_lo, o_hi


def _token_shift(p, prev_last, mu):
    rolled = pltpu.roll(p, 1, 0)
    row = lax.broadcasted_iota(jnp.int32, p.shape, 0)
    p_prev = jnp.where(row == 0, prev_last, rolled)
    return p + mu * (p_prev - p)


def _head_sums(x, ones_ref):
    tiles = [jnp.dot(x[:, c:c + V7X_LANES].astype(BF16), ones_ref[...], preferred_element_type=F32)
             for c in range(0, x.shape[1], V7X_LANES)]
    return jnp.concatenate(tiles, axis=1)


def _rwkv_prep_body(*refs, has_vres):
    if has_vres:
        (pm_ref, pz_ref, mum_ref, muz_ref, w0_ref, wl_ref, a0_ref, al_ref, gl_ref, kk_ref, ka_ref, rk_ref,
         ones_ref, v0_ref, vl_ref, vf_ref,
         r_ref, ld_ref, k_ref, v_ref, kn_ref, b_ref, bonus_ref, g_ref, cm_ref, cz_ref) = refs
    else:
        (pm_ref, pz_ref, mum_ref, muz_ref, w0_ref, wl_ref, a0_ref, al_ref, gl_ref, kk_ref, ka_ref, rk_ref,
         ones_ref,
         r_ref, ld_ref, k_ref, v_ref, kn_ref, b_ref, bonus_ref, g_ref, cm_ref, cz_ref) = refs
    tm = pm_ref.shape[1]
    bw = B_WIDTH

    @pl.when(pl.program_id(1) == 0)
    def _():
        cm_ref[...] = jnp.zeros_like(cm_ref)
        cz_ref[...] = jnp.zeros_like(cz_ref)

    pm = pm_ref[0].astype(F32)
    pz = pz_ref[0]
    last = V7X_SUBLANES - 1
    pms = _token_shift(pm, cm_ref[last:last + 1, :], mum_ref[...])
    pzs = _token_shift(pz, cz_ref[last:last + 1, :], muz_ref[...])
    cm_ref[...] = pm[tm - V7X_SUBLANES:, :]
    cz_ref[...] = pz[tm - V7X_SUBLANES:, :]

    r = pms[:, :bw]
    k = pms[:, bw:2 * bw]
    v = pms[:, 2 * bw:3 * bw]

    z_w = jnp.tanh(pzs[:, ZW_OFF:ZW_OFF + ZW_PAD]).astype(BF16)
    w = w0_ref[...] + jnp.dot(z_w, wl_ref[...], preferred_element_type=F32)
    w = -jax.nn.softplus(-w) - 0.5
    ld_ref[0] = -jnp.exp(w)

    z_a = pzs[:, ZA_OFF:ZA_OFF + ZA_PAD].astype(BF16)
    a = jax.nn.sigmoid(a0_ref[...] + jnp.dot(z_a, al_ref[...], preferred_element_type=F32))

    z_g = jax.nn.sigmoid(pzs[:, ZG_OFF:ZG_OFF + ZG_PAD]).astype(BF16)
    g_ref[0] = jnp.dot(z_g, gl_ref[...], preferred_element_type=F32).astype(g_ref.dtype)

    if has_vres:
        z_v = pzs[:, ZV_OFF:ZV_OFF + ZV_PAD].astype(BF16)
        mix = jax.nn.sigmoid(v0_ref[...] + jnp.dot(z_v, vl_ref[...], preferred_element_type=F32))
        v = v + (vf_ref[0].astype(F32) - v) * mix

    kk = k * kk_ref[...]
    kk = kk * jnp.minimum(lax.rsqrt(_head_sums(kk * kk, ones_ref)), 1e12)
    k = k * (1.0 + (a - 1.0) * ka_ref[...])
    r_ref[0] = r.astype(r_ref.dtype)
    k_ref[0] = k.astype(k_ref.dtype)
    v_ref[0] = v.astype(v_ref.dtype)
    kn_ref[0] = kk.astype(kn_ref.dtype)
    b_ref[0] = (kk * a).astype(b_ref.dtype)
    bonus_ref[0] = (_head_sums(r * k * rk_ref[...], ones_ref) * v).astype(bonus_ref.dtype)


def _pad_rows(w, rows):
    return jnp.pad(w, ((0, rows - w.shape[0]), (0, 0)))


def _head_pair_ones():
    head = jnp.arange(V7X_LANES) // B_HEAD_DIM
    return (head[:, None] == head[None, :]).astype(BF16)


def rwkv_prep(proj, pz, shift_mu, w0, w_lora, a0, a_lora, g_lora, k_k, k_a, r_k, v0, v_lora, v_first, name):
    b, s, main_blk = proj.shape
    bw = B_WIDTH
    has_vres = v_lora is not None
    tm = _pick(s, (256, 128))
    assert main_blk == 3 * bw

    mu_main = shift_mu[:3 * bw].reshape(1, 3 * bw)
    lora_mu = shift_mu[3 * bw:]
    mu_z = jnp.zeros((Z_WIDTH,), F32)
    mu_z = mu_z.at[ZW_OFF:ZW_OFF + LORA_W].set(lora_mu[:LORA_W])
    mu_z = mu_z.at[ZA_OFF:ZA_OFF + LORA_A].set(lora_mu[LORA_W:LORA_W + LORA_A])
    mu_z = mu_z.at[ZG_OFF:ZG_OFF + LORA_G].set(lora_mu[LORA_W + LORA_A:LORA_W + LORA_A + LORA_G])
    if has_vres:
        mu_z = mu_z.at[ZV_OFF:ZV_OFF + LORA_V].set(lora_mu[LORA_W + LORA_A + LORA_G:])
    mu_z = mu_z.reshape(1, Z_WIDTH)

    row = lambda t: t.reshape(1, bw)
    const = lambda shape: pl.BlockSpec(shape, lambda bi, i: (0, 0))
    tok = pl.BlockSpec((1, tm, bw), lambda bi, i: (bi, i, 0))
    args = [proj, pz, mu_main, mu_z, row(w0), _pad_rows(w_lora, ZW_PAD).astype(BF16),
            row(a0), _pad_rows(a_lora, ZA_PAD).astype(BF16), _pad_rows(g_lora, ZG_PAD).astype(BF16),
            row(k_k), row(k_a), row(r_k), _head_pair_ones()]
    in_specs = [
        pl.BlockSpec((1, tm, main_blk), lambda bi, i: (bi, i, 0)),
        pl.BlockSpec((1, tm, Z_WIDTH), lambda bi, i: (bi, i, 0)),
        const((1, main_blk)), const((1, Z_WIDTH)),
        const((1, bw)), const((ZW_PAD, bw)), const((1, bw)), const((ZA_PAD, bw)), const((ZG_PAD, bw)),
        const((1, bw)), const((1, bw)), const((1, bw)), const((V7X_LANES, V7X_LANES)),
    ]
    if has_vres:
        args += [row(v0), _pad_rows(v_lora, ZV_PAD).astype(BF16), v_first]
        in_specs += [const((1, bw)), const((ZV_PAD, bw)), tok]
    out_dtypes = [BF16, F32, BF16, BF16, BF16, BF16, BF16, BF16]
    n_out = len(out_dtypes)
    est = (2 * _nbytes((tm, main_blk + Z_WIDTH), F32) + (2 * n_out + 6) * _nbytes((tm, bw), F32)
           + 6 * _nbytes((tm, main_blk), F32) + 4 * _nbytes((ZG_PAD + 3 * ZW_PAD, bw), BF16))
    return pl.pallas_call(
        functools.partial(_rwkv_prep_body, has_vres=has_vres),
        grid=(b, s // tm),
        in_specs=in_specs,
        out_specs=[tok] * n_out,
        out_shape=[jax.ShapeDtypeStruct((b, s, bw), dt) for dt in out_dtypes],
        scratch_shapes=[pltpu.VMEM((V7X_SUBLANES, main_blk), F32), pltpu.VMEM((V7X_SUBLANES, Z_WIDTH), F32)],
        compiler_params=pltpu.CompilerParams(
            dimension_semantics=("parallel", "arbitrary"), vmem_limit_bytes=_vmem_limit(est)),
        name=name,
    )(*args)


def _cumsum_rows(x):
    row = lax.broadcasted_iota(jnp.int32, x.shape, 0)
    sh = 1
    while sh < x.shape[0]:
        x = x + jnp.where(row >= sh, pltpu.roll(x, sh, 0), 0.0)
        sh *= 2
    return x


def _stack_heads(x, m0, m1):
    return jnp.concatenate([x * m0, x * m1], axis=0)


def _nt_dot(a, b):
    return lax.dot_general(a.astype(BF16), b.astype(BF16), (((1,), (1,)), ((), ())), preferred_element_type=F32)


def _tn_dot(a, b):
    return lax.dot_general(a.astype(BF16), b.astype(BF16), (((0,), (0,)), ((), ())), preferred_element_type=F32)


def _nn_dot(a, b):
    return jnp.dot(a.astype(BF16), b.astype(BF16), preferred_element_type=F32)


def _rwkv_chunk(streams, hts, masks):
    pairs = range(len(streams))
    c = streams[0][0].shape[0]
    n2 = 2 * c
    m0, m1, strict, incl, eye, levels = masks
    ar, bk, v_st, p_last = [], [], [], []
    for r, ld, k, v, kk, b in streams:
        cs = _cumsum_rows(ld)
        p = jnp.exp(cs)
        p_inv = jnp.exp(-cs)
        a_st = _stack_heads(-kk * jnp.exp(cs - ld), m0, m1)
        r_st = _stack_heads(r * p, m0, m1)
        b_st = _stack_heads(b * p_inv, m0, m1)
        k_st = _stack_heads(k * p_inv, m0, m1)
        ar.append(jnp.concatenate([a_st, r_st], axis=0).astype(BF16))
        bk.append(jnp.concatenate([b_st, k_st], axis=0).astype(BF16))
        v_st.append(_stack_heads(v, m0, m1))
        p_last.append(p[c - 1:c, :])

    q = [_nt_dot(ar[i], bk[i]) for i in pairs]
    l_ab = [q[i][:n2, :n2] * strict for i in pairs]
    t = [eye + l_ab[i] * levels[0] for i in pairs]
    for lvl in levels[1:]:
        tc = [_nn_dot(t[i], l_ab[i] * lvl) for i in pairs]
        t = [t[i] + _nn_dot(tc[i], t[i]) for i in pairs]

    ar_h = [_nt_dot(ar[i], hts[i]) for i in pairs]
    ak_v = [_nn_dot(q[i][:n2, n2:] * strict, v_st[i]) for i in pairs]
    u_st = [_nn_dot(t[i], ar_h[i][:n2] + ak_v[i]) for i in pairs]
    uv = [jnp.concatenate([u_st[i], v_st[i]], axis=0).astype(BF16) for i in pairs]
    y_st = [ar_h[i][n2:] + _nn_dot(q[i][n2:, :] * jnp.concatenate([incl, incl], axis=1), uv[i]) for i in pairs]
    ys = [y_st[i][:c] + y_st[i][c:] for i in pairs]
    ht_new = [(hts[i] + _tn_dot(uv[i], bk[i])) * p_last[i] for i in pairs]
    return ys, ht_new


def _rwkv_scan_body(r_ref, ld_ref, k_ref, v_ref, kk_ref, b_ref, bonus_ref, g_ref, gw_ref, gb_ref, ones_ref,
                    o_ref, ht_ref):
    c = r_ref.shape[1]
    n2 = 2 * c
    lanes = V7X_LANES

    @pl.when(pl.program_id(1) == 0)
    def _():
        ht_ref[...] = jnp.zeros_like(ht_ref)

    lane = lax.broadcasted_iota(jnp.int32, (1, lanes), 1)
    m0 = jnp.where(lane < B_HEAD_DIM, 1.0, 0.0)
    m1 = 1.0 - m0
    row = lax.broadcasted_iota(jnp.int32, (n2, n2), 0)
    col = lax.broadcasted_iota(jnp.int32, (n2, n2), 1)
    log2c = c.bit_length() - 1
    same = jnp.right_shift(row, log2c) == jnp.right_shift(col, log2c)
    strict = jnp.where(same & (row > col), 1.0, 0.0)
    incl = jnp.where(same & (row >= col), 1.0, 0.0)
    eye = jnp.where(row == col, 1.0, 0.0)
    levels = []
    for lh in range(log2c):
        blk = jnp.right_shift(row, lh + 1) == jnp.right_shift(col, lh + 1)
        lower_left = (jnp.bitwise_and(jnp.right_shift(row, lh), 1) == 1) & \
                     (jnp.bitwise_and(jnp.right_shift(col, lh), 1) == 0)
        levels.append(jnp.where(blk & lower_left, 1.0, 0.0))
    masks = (m0, m1, strict, incl, eye, levels)

    where = [(bi, slice(i * lanes, (i + 1) * lanes))
             for bi in range(r_ref.shape[0]) for i in range(r_ref.shape[2] // lanes)]
    streams = [tuple(ref[bi, :, cs].astype(F32) for ref in (r_ref, ld_ref, k_ref, v_ref, kk_ref, b_ref))
               for bi, cs in where]
    ys, ht_new = _rwkv_chunk(streams, [ht_ref[i] for i in range(len(where))], masks)
    for i in range(len(where)):
        ht_ref[i] = ht_new[i]
    inv_n = 1.0 / B_HEAD_DIM
    means = [jnp.dot(y.astype(BF16), ones_ref[...], preferred_element_type=F32) * inv_n for y in ys]
    ds = [y - mean for y, mean in zip(ys, means)]
    vrs = [jnp.dot((d * d).astype(BF16), ones_ref[...], preferred_element_type=F32) * inv_n for d in ds]
    for (bi, cs), d, var in zip(where, ds, vrs):
        yn = d * lax.rsqrt(var + GN_EPS) * gw_ref[:, cs] + gb_ref[:, cs]
        o_ref[bi, :, cs] = ((yn + bonus_ref[bi, :, cs].astype(F32)) * g_ref[bi, :, cs].astype(F32)
                            ).astype(o_ref.dtype)


def rwkv_scan(r, ld, k, v, kk, b, bonus, g, gn_w, gn_b, name):
    bsz, s, bw = r.shape
    c = RWKV_CHUNK
    nb = _pick(bsz, (RWKV_SEQS_PER_STEP, 1))
    assert s % c == 0 and c & (c - 1) == 0 and 2 * B_HEAD_DIM == V7X_LANES
    tok = pl.BlockSpec((nb, c, bw), lambda bi, i: (bi, i, 0))
    par = pl.BlockSpec((1, bw), lambda bi, i: (0, 0))
    chains = nb * (bw // V7X_LANES)
    est = (18 * nb * _nbytes((c, bw), F32) + chains * _nbytes((V7X_LANES, V7X_LANES), F32)
           + 8 * chains * _nbytes((4 * c, 4 * c), F32))
    return pl.pallas_call(
        _rwkv_scan_body,
        grid=(bsz // nb, s // c),
        in_specs=[tok] * 8 + [par] * 2 + [pl.BlockSpec((V7X_LANES, V7X_LANES), lambda bi, i: (0, 0))],
        out_specs=tok,
        out_shape=jax.ShapeDtypeStruct((bsz, s, bw), BF16),
        scratch_shapes=[pltpu.VMEM((chains, V7X_LANES, V7X_LANES), F32)],
        compiler_params=pltpu.CompilerParams(
            dimension_semantics=("parallel", "arbitrary"), vmem_limit_bytes=_vmem_limit(est)),
        name=name,
    )(r, ld, k, v, kk, b, bonus, g, gn_w.reshape(1, bw), gn_b.reshape(1, bw), _head_pair_ones())


def _lora_proj_weight(w_in, has_vres):
    o = 3 * A_WIDTH + 3 * B_WIDTH
    d = w_in.shape[0]
    wz = jnp.zeros((d, Z_WIDTH), w_in.dtype)
    wz = wz.at[:, ZW_OFF:ZW_OFF + LORA_W].set(w_in[:, o:o + LORA_W])
    o += LORA_W
    wz = wz.at[:, ZA_OFF:ZA_OFF + LORA_A].set(w_in[:, o:o + LORA_A])
    o += LORA_A
    wz = wz.at[:, ZG_OFF:ZG_OFF + LORA_G].set(w_in[:, o:o + LORA_G])
    o += LORA_G
    if has_vres:
        wz = wz.at[:, ZV_OFF:ZV_OFF + LORA_V].set(w_in[:, o:o + LORA_V])
    return wz


def moba_rwkv_layer(x, b, s, li, norm_mix, w_in, q_gain, k_gain, shift_mu, w0, w_lora, a0, a_lora, g_lora,
                    k_k, k_a, r_k, gn_w, gn_b, w_out, v0, v_lora, v_first):
    has_vres = v_lora is not None
    tag = "l%d_" % li
    n_main = 3 * A_WIDTH + 3 * B_WIDTH
    proj_a, proj_b, pz = norm_matmul_split(
        x, norm_mix, w_in.astype(BF16), n_main, 3 * A_WIDTH, _lora_proj_weight(w_in, has_vres).astype(BF16),
        F32, BF16, F32, tag + "in_proj")
    pz = pz.reshape(b, s, Z_WIDTH)

    ya_lo, ya_hi = moba_attention(proj_a.reshape(b, s, 3 * A_WIDTH), q_gain, k_gain, tag + "moba")

    r, ld, k, v, kk, kb, bonus, g = rwkv_prep(proj_b.reshape(b, s, 3 * B_WIDTH), pz, shift_mu, w0, w_lora, a0,
                                              a_lora, g_lora, k_k, k_a, r_k.reshape(-1), v0, v_lora, v_first,
                                              tag + "rwkv_prep")
    if not has_vres:
        v_first = v
    y_b = rwkv_scan(r, ld, k, v, kk, kb, bonus, g, gn_w, gn_b, tag + "rwkv_scan")

    x = mix_out_proj(ya_lo.reshape(b * s // 2, A_WIDTH), ya_hi.reshape(b * s // 2, A_WIDTH),
                     y_b.reshape(b * s, B_WIDTH), w_out.astype(BF16), x, s, tag + "out_proj")
    return x, v_first


def short_conv_layer(x, b, s, li, norm_mix, conv_in, conv_w, conv_out):
    tag = "l%d_" % li
    m = norm_matmul_conv(x, norm_mix, conv_in.astype(BF16), conv_w, s, 3, _short_conv_combine, tag + "conv_in")
    return matmul_residual(m, conv_out.astype(BF16), x, tag + "conv_out")


def conv_ffn(x, b, s, li, norm_ffn, ffn_up, ffn_conv, ffn_down):
    tag = "l%d_" % li
    m = norm_matmul_conv(x, norm_ffn, ffn_up.astype(BF16), ffn_conv, s, 2, _ffn_combine, tag + "ffn_up")
    return matmul_residual(m, ffn_down.astype(BF16), x, tag + "ffn_down")


def kernel(x, l0_norm_mix, l0_w_in, l0_q_gain, l0_k_gain, l0_shift_mu, l0_w0, l0_w_lora, l0_a0, l0_a_lora, l0_g_lora, l0_k_k, l0_k_a, l0_r_k, l0_gn_w, l0_gn_b, l0_w_out, l0_norm_ffn, l0_ffn_up, l0_ffn_conv, l0_ffn_down, l1_norm_mix, l1_conv_in, l1_conv_w, l1_conv_out, l1_norm_ffn, l1_ffn_up, l1_ffn_conv, l1_ffn_down, l2_norm_mix, l2_w_in, l2_q_gain, l2_k_gain, l2_shift_mu, l2_w0, l2_w_lora, l2_a0, l2_a_lora, l2_g_lora, l2_k_k, l2_k_a, l2_r_k, l2_gn_w, l2_gn_b, l2_w_out, l2_v0, l2_v_lora, l2_norm_ffn, l2_ffn_up, l2_ffn_conv, l2_ffn_down, l3_norm_mix, l3_conv_in, l3_conv_w, l3_conv_out, l3_norm_ffn, l3_ffn_up, l3_ffn_conv, l3_ffn_down):
    b, s, d = x.shape
    h = x.reshape(b * s, d)

    h, v_first = moba_rwkv_layer(
        h, b, s, 0, l0_norm_mix, l0_w_in, l0_q_gain, l0_k_gain, l0_shift_mu, l0_w0, l0_w_lora, l0_a0,
        l0_a_lora, l0_g_lora, l0_k_k, l0_k_a, l0_r_k, l0_gn_w, l0_gn_b, l0_w_out, None, None, None)
    h = conv_ffn(h, b, s, 0, l0_norm_ffn, l0_ffn_up, l0_ffn_conv, l0_ffn_down)

    h = short_conv_layer(h, b, s, 1, l1_norm_mix, l1_conv_in, l1_conv_w, l1_conv_out)
    h = conv_ffn(h, b, s, 1, l1_norm_ffn, l1_ffn_up, l1_ffn_conv, l1_ffn_down)

    h, _ = moba_rwkv_layer(
        h, b, s, 2, l2_norm_mix, l2_w_in, l2_q_gain, l2_k_gain, l2_shift_mu, l2_w0, l2_w_lora, l2_a0,
        l2_a_lora, l2_g_lora, l2_k_k, l2_k_a, l2_r_k, l2_gn_w, l2_gn_b, l2_w_out, l2_v0, l2_v_lora, v_first)
    h = conv_ffn(h, b, s, 2, l2_norm_ffn, l2_ffn_up, l2_ffn_conv, l2_ffn_down)

    h = short_conv_layer(h, b, s, 3, l3_norm_mix, l3_conv_in, l3_conv_w, l3_conv_out)
    h = conv_ffn(h, b, s, 3, l3_norm_ffn, l3_ffn_up, l3_ffn_conv, l3_ffn_down)
    return h.reshape(b, s, d)
```

```python
import functools

import jax
import jax.numpy as jnp
from jax import lax
from jax.experimental import pallas as pl
from jax.experimental.pallas import tpu as pltpu

F32 = jnp.float32
BF16 = jnp.bfloat16

A_HEADS = 8
A_HEAD_DIM = 128
A_WIDTH = A_HEADS * A_HEAD_DIM
MOBA_BLOCK = 256
MOBA_TOPK = 3
MOBA_HEADS_PER_STEP = 4
MOBA_DENOM_ROWS = 16
B_HEADS = 16
B_HEAD_DIM = 64
B_WIDTH = B_HEADS * B_HEAD_DIM
RWKV_CHUNK = 64
RWKV_SEQS_PER_STEP = 2
LORA_W = 64
LORA_A = 64
LORA_G = 160
LORA_V = 32
CONV_WIDTH = 3
RMS_EPS = 1e-6
GN_EPS = 64e-5
NEG_INF = -1e30
LOG2E = 1.4426950408889634

V7X_LANES = 128
V7X_SUBLANES = 8
V7X_VMEM_BYTES = 64 * 1024 * 1024
V7X_VMEM_CAP = V7X_VMEM_BYTES - 8 * 1024 * 1024
SPILL_ALLOWANCE = 4 * 1024 * 1024
CONV_ROW_CHUNK = 256

ZW_OFF, ZW_PAD = 0, 128
ZA_OFF, ZA_PAD = 128, 128
ZG_OFF, ZG_PAD = 256, 256
ZV_OFF, ZV_PAD = 512, 128
Z_WIDTH = 640


def _vmem_limit(nbytes):
    return int(min(max(nbytes + SPILL_ALLOWANCE, 16 * 1024 * 1024), V7X_VMEM_CAP))


def _nbytes(shape, dtype):
    n = 1
    for s in shape:
        n *= s
    return n * jnp.dtype(dtype).itemsize


def _pick(n, prefs):
    for p in prefs:
        if n % p == 0:
            return p
    return n


def _norm_matmul_split_body(x_ref, g_ref, w_ref, wz_ref, oa_ref, ob_ref, oz_ref, xn_ref, *, na, nt):
    j = pl.program_id(1)

    @pl.when(j == 0)
    def _():
        x = x_ref[...]
        ms = jnp.mean(x * x, axis=-1, keepdims=True)
        xn = (x * lax.rsqrt(ms + RMS_EPS) * g_ref[...]).astype(BF16)
        xn_ref[...] = xn
        oz_ref[...] = jnp.dot(xn, wz_ref[...], preferred_element_type=F32).astype(oz_ref.dtype)

    @pl.when(j > 0)
    def _():
        h = jnp.dot(xn_ref[...], w_ref[...], preferred_element_type=F32)

        @pl.when(j <= na)
        def _():
            oa_ref[...] = h.astype(oa_ref.dtype)

        @pl.when(j > na)
        def _():
            ob_ref[...] = h.astype(ob_ref.dtype)


def norm_matmul_split(x, gain, w, n, n_a, wz, dtype_a, dtype_b, dtype_z, name):
    m, k = x.shape
    nz = wz.shape[1]
    tm = _pick(m, (1024, 512, 256, 128))
    tn = _pick(n_a, (1024, 512, 256, 128))
    assert n % tn == 0
    na = n_a // tn
    nt = n // tn
    est = (2 * _nbytes((tm, k), F32) + _nbytes((tm, k), BF16) + 2 * _nbytes((k, tn + nz), BF16)
           + 8 * _nbytes((tm, tn), F32) + 3 * _nbytes((tm, nz), F32) + _nbytes((tm, k), F32))
    return pl.pallas_call(
        functools.partial(_norm_matmul_split_body, na=na, nt=nt),
        grid=(m // tm, nt + 1),
        in_specs=[
            pl.BlockSpec((tm, k), lambda i, j: (i, 0)),
            pl.BlockSpec((1, k), lambda i, j: (0, 0)),
            pl.BlockSpec((k, tn), lambda i, j: (0, jnp.maximum(j - 1, 0))),
            pl.BlockSpec((k, nz), lambda i, j: (0, 0)),
        ],
        out_specs=[
            pl.BlockSpec((tm, tn), lambda i, j: (i, jnp.clip(j - 1, 0, na - 1))),
            pl.BlockSpec((tm, tn), lambda i, j: (i, jnp.clip(j - 1 - na, 0, nt - na - 1))),
            pl.BlockSpec((tm, nz), lambda i, j: (i, 0)),
        ],
        out_shape=[jax.ShapeDtypeStruct((m, n_a), dtype_a), jax.ShapeDtypeStruct((m, n - n_a), dtype_b),
                   jax.ShapeDtypeStruct((m, nz), dtype_z)],
        scratch_shapes=[pltpu.VMEM((tm, k), BF16)],
        compiler_params=pltpu.CompilerParams(
            dimension_semantics=("arbitrary", "arbitrary"), vmem_limit_bytes=_vmem_limit(est)),
        name=name,
    )(x, gain.reshape(1, k), w, wz)


def _conv3_ext(u, w):
    return w[0:1, :] * pltpu.roll(u, 2, 0) + w[1:2, :] * pltpu.roll(u, 1, 0) + w[2:3, :] * u


def _norm_matmul_conv_body(x_ref, g_ref, *refs, n_groups, n_taps, tiles_per_seq, combine):
    w_refs = refs[:n_groups]
    cw_refs = refs[n_groups:n_groups + n_taps]
    o_ref, xn_ref, carry_ref = refs[n_groups + n_taps:]
    i = pl.program_id(0)
    j = pl.program_id(1)
    tm, tn = o_ref.shape
    sub = V7X_SUBLANES
    rc = min(tm, CONV_ROW_CHUNK)

    @pl.when(j == 0)
    def _():
        x = x_ref[...]
        ms = jnp.mean(x * x, axis=-1, keepdims=True)
        xn_ref[...] = (x * lax.rsqrt(ms + RMS_EPS) * g_ref[...]).astype(BF16)

    @pl.when(i % tiles_per_seq == 0)
    def _():
        carry_ref[j] = jnp.zeros(carry_ref.shape[1:], F32)

    w = jnp.concatenate([w_ref[...] for w_ref in w_refs], axis=1)
    h = jnp.dot(xn_ref[...], w, preferred_element_type=F32)
    taps = [cw_ref[...] for cw_ref in cw_refs]
    for c in range(tm // rc):
        r0 = c * rc
        if c == 0:
            ext = jnp.concatenate([carry_ref[j], h[:rc, :]], axis=0)
        else:
            ext = h[r0 - sub:r0 + rc, :]
        groups = [ext[:, g * tn:(g + 1) * tn] for g in range(n_groups)]
        o_ref[r0:r0 + rc, :] = combine(groups, taps)[sub:, :].astype(o_ref.dtype)
    carry_ref[j] = h[tm - sub:, :]


def _ffn_combine(groups, taps):
    return jax.nn.silu(_conv3_ext(groups[0], taps[0])) * _conv3_ext(groups[1], taps[1])


def _short_conv_combine(groups, taps):
    return groups[0] * _conv3_ext(groups[1] * groups[2], taps[0])


def norm_matmul_conv(x, gain, w, conv_w, seq, n_groups, combine, name):
    m, k = x.shape
    n = w.shape[1] // n_groups
    tm = _pick(seq, (1024, 512, 256, 128))
    tn = _pick(n, (512, 256, 128))
    nj = n // tn
    n_cw = conv_w.shape[1] // n
    est = (2 * _nbytes((tm, k), F32) + _nbytes((tm, k), BF16) + 3 * n_groups * _nbytes((k, tn), BF16)
           + 2 * _nbytes((tm, tn), BF16) + (n_groups + 2) * _nbytes((tm, tn), F32)
           + 8 * n_groups * _nbytes((CONV_ROW_CHUNK, tn), F32) + nj * n_groups * _nbytes((V7X_SUBLANES, tn), F32))
    w_specs = [pl.BlockSpec((k, tn), lambda i, j, g=g: (0, g * nj + j)) for g in range(n_groups)]
    cw_specs = [pl.BlockSpec((CONV_WIDTH, tn), lambda i, j, g=g: (0, g * nj + j)) for g in range(n_cw)]
    return pl.pallas_call(
        functools.partial(_norm_matmul_conv_body, n_groups=n_groups, n_taps=n_cw, tiles_per_seq=seq // tm,
                          combine=combine),
        grid=(m // tm, nj),
        in_specs=[pl.BlockSpec((tm, k), lambda i, j: (i, 0)), pl.BlockSpec((1, k), lambda i, j: (0, 0))]
        + w_specs + cw_specs,
        out_specs=pl.BlockSpec((tm, tn), lambda i, j: (i, j)),
        out_shape=jax.ShapeDtypeStruct((m, n), BF16),
        scratch_shapes=[pltpu.VMEM((tm, k), BF16),
                        pltpu.VMEM((nj, V7X_SUBLANES, n_groups * tn), F32)],
        compiler_params=pltpu.CompilerParams(
            dimension_semantics=("arbitrary", "arbitrary"), vmem_limit_bytes=_vmem_limit(est)),
        name=name,
    )(x, gain.reshape(1, k), *([w] * n_groups), *([conv_w] * n_cw))


def _matmul_res_body(a_ref, w_ref, r_ref, o_ref):
    o_ref[...] = r_ref[...] + jnp.dot(a_ref[...], w_ref[...], preferred_element_type=F32)


def matmul_residual(a, w, res, name):
    m, k = a.shape
    n = w.shape[1]
    tm = _pick(m, (1024, 512, 256, 128))
    tn = _pick(n, (512, 256, 128)) if k > 4096 else _pick(n, (1024, 512, 256, 128))
    est = (2 * _nbytes((tm, k), BF16) + 2 * _nbytes((k, tn), BF16) + 6 * _nbytes((tm, tn), F32))
    return pl.pallas_call(
        _matmul_res_body,
        grid=(n // tn, m // tm),
        in_specs=[
            pl.BlockSpec((tm, k), lambda j, i: (i, 0)),
            pl.BlockSpec((k, tn), lambda j, i: (0, j)),
            pl.BlockSpec((tm, tn), lambda j, i: (i, j)),
        ],
        out_specs=pl.BlockSpec((tm, tn), lambda j, i: (i, j)),
        out_shape=jax.ShapeDtypeStruct((m, n), F32),
        compiler_params=pltpu.CompilerParams(
            dimension_semantics=("parallel", "parallel"), vmem_limit_bytes=_vmem_limit(est)),
        name=name,
    )(a, w, res)


def _mix_out_body(ylo_ref, yhi_ref, yb_ref, wa_ref, wb_ref, r_ref, o_ref):
    half = ylo_ref.shape[0]
    for rows, ya_ref in ((slice(0, half), ylo_ref), (slice(half, 2 * half), yhi_ref)):
        acc = jnp.dot(ya_ref[...], wa_ref[...], preferred_element_type=F32)
        acc = acc + jnp.dot(yb_ref[rows, :], wb_ref[...], preferred_element_type=F32)
        o_ref[rows, :] = r_ref[rows, :] + acc


def mix_out_proj(ya_lo, ya_hi, yb, w_out, res, seq, name):
    m, kb = yb.shape
    ka = ya_lo.shape[1]
    n = w_out.shape[1]
    half = seq // 2
    tn = _pick(n, (512, 256, 128))
    est = (4 * _nbytes((half, ka), BF16) + 2 * _nbytes((seq, kb), BF16) + 2 * _nbytes((ka + kb, tn), BF16)
           + 5 * _nbytes((seq, tn), F32))
    return pl.pallas_call(
        _mix_out_body,
        grid=(m // seq, n // tn),
        in_specs=[
            pl.BlockSpec((half, ka), lambda i, j: (i, 0)),
            pl.BlockSpec((half, ka), lambda i, j: (i, 0)),
            pl.BlockSpec((seq, kb), lambda i, j: (i, 0)),
            pl.BlockSpec((ka, tn), lambda i, j: (0, j)),
            pl.BlockSpec((kb, tn), lambda i, j: (0, j)),
            pl.BlockSpec((seq, tn), lambda i, j: (i, j)),
        ],
        out_specs=pl.BlockSpec((seq, tn), lambda i, j: (i, j)),
        out_shape=jax.ShapeDtypeStruct((m, n), F32),
        compiler_params=pltpu.CompilerParams(
            dimension_semantics=("parallel", "parallel"), vmem_limit_bytes=_vmem_limit(est)),
        name=name,
    )(ya_lo, ya_hi, yb, w_out[:ka], w_out[ka:], res)


def _head_rms(x, gain):
    return x * lax.rsqrt(jnp.mean(x * x, axis=-1, keepdims=True) + RMS_EPS) * gain


def _moba_query_block(ii, q, q_gain, slope, rel_t, kn_ref, vt_ref, km_ref):
    bs = MOBA_BLOCK
    hd = A_HEAD_DIM
    qn = _head_rms(q, q_gain)
    qs = (qn * (hd ** -0.5 * LOG2E)).astype(BF16)
    slope2 = slope * LOG2E
    bias_rel = slope2 * rel_t
    gt = lax.dot_general(km_ref[...], qn, (((1,), (1,)), ((), ())),
                         precision=lax.Precision.HIGHEST, preferred_element_type=F32)
    blk = lax.broadcasted_iota(jnp.int32, gt.shape, 0)
    gm = jnp.where(blk < ii, gt, NEG_INF)
    scores = []
    m_all = None
    for n in range(ii + 1):
        s = lax.dot_general(kn_ref[n * bs:(n + 1) * bs, :], qs, (((1,), (1,)), ((), ())),
                            preferred_element_type=F32) - bias_rel
        if n == ii:
            s = jnp.where(rel_t >= 0, s, NEG_INF)
            shift = jnp.zeros((1, bs), F32)
        else:
            g_n = gm[n:n + 1, :]
            beats = jnp.where((gm > g_n) | ((gm == g_n) & (blk < n)), 1.0, 0.0)
            chosen = jnp.sum(beats, axis=0, keepdims=True) < float(MOBA_TOPK)
            shift = jnp.where(chosen, slope2 * float((ii - n) * bs), -NEG_INF)
        m_blk = jnp.max(s, axis=0, keepdims=True) - shift
        m_all = m_blk if m_all is None else jnp.maximum(m_all, m_blk)
        scores.append((s, shift))
    acc = None
    for n, (s, shift) in enumerate(scores):
        p = jnp.exp2(s - (m_all + shift)).astype(BF16)
        pv = jnp.dot(vt_ref[:, n * bs:(n + 1) * bs], p, preferred_element_type=F32)
        acc = pv if acc is None else acc + pv
    return (acc[:hd] / acc[hd:hd + 1]).T


def _moba_body(qlo_ref, qhi_ref, k_ref, v_ref, qg_ref, kg_ref, olo_ref, ohi_ref, kn_ref, vt_ref, km_ref,
               *, n_heads, n_blocks):
    bs = MOBA_BLOCK
    hd = A_HEAD_DIM
    hp = pl.program_id(1)
    j = pl.program_id(2)

    @pl.when(j == 0)
    def _():
        km_ref[...] = jnp.zeros_like(km_ref)
        for e in range(MOBA_HEADS_PER_STEP):
            cols = slice(e * hd, (e + 1) * hd)
            for nb in range(n_blocks):
                rows = slice(nb * bs, (nb + 1) * bs)
                kn = _head_rms(k_ref[0, rows, cols], kg_ref[...])
                kn_ref[e, rows, :] = kn.astype(BF16)
                km_ref[e, nb:nb + 1, :] = jnp.mean(kn, axis=0, keepdims=True)
                vt_ref[e, :hd, rows] = v_ref[0, rows, cols].T.astype(BF16)
            vt_ref[e, hd:, :] = jnp.ones((MOBA_DENOM_ROWS, vt_ref.shape[2]), BF16)

    rel_t = (lax.broadcasted_iota(jnp.int32, (bs, bs), 1)
             - lax.broadcasted_iota(jnp.int32, (bs, bs), 0)).astype(F32)

    for jj in range(n_blocks // 2):
        @pl.when(j == jj)
        def _(jj=jj):
            for e in range(MOBA_HEADS_PER_STEP):
                cols = slice(e * hd, (e + 1) * hd)
                head = (hp * MOBA_HEADS_PER_STEP + e + 1).astype(F32)
                slope = jnp.exp2(jnp.full((1, bs), -8.0 / n_heads, F32) * head)
                for q_ref, o_ref, ii in ((qlo_ref, olo_ref, jj), (qhi_ref, ohi_ref, n_blocks - 1 - jj)):
                    o = _moba_query_block(ii, q_ref[0, :, cols], qg_ref[...], slope, rel_t,
                                          kn_ref.at[e], vt_ref.at[e], km_ref.at[e])
                    o_ref[0, :, cols] = o.astype(o_ref.dtype)


def moba_attention(proj, q_gain, k_gain, name):
    b, s, _ = proj.shape
    assert s % MOBA_BLOCK == 0
    nb = s // MOBA_BLOCK
    assert nb % 2 == 0 and A_HEADS % MOBA_HEADS_PER_STEP == 0
    nbp = -(-nb // V7X_SUBLANES) * V7X_SUBLANES
    hd = A_HEAD_DIM
    hw = MOBA_HEADS_PER_STEP * hd
    groups = A_HEADS // MOBA_HEADS_PER_STEP
    half = nb // 2
    est = (8 * _nbytes((MOBA_BLOCK, hw), F32) + 4 * _nbytes((s, hw), F32) + 2 * _nbytes((s, hw), BF16)
           + MOBA_HEADS_PER_STEP * (4 * nb + 16) * _nbytes((MOBA_BLOCK, MOBA_BLOCK), F32))
    o_lo, o_hi = pl.pallas_call(
        functools.partial(_moba_body, n_heads=A_HEADS, n_blocks=nb),
        grid=(b, groups, half),
        in_specs=[
            pl.BlockSpec((1, MOBA_BLOCK, hw), lambda bi, g, j: (bi, j, g)),
            pl.BlockSpec((1, MOBA_BLOCK, hw), lambda bi, g, j: (bi, nb - 1 - j, g)),
            pl.BlockSpec((1, s, hw), lambda bi, g, j: (bi, 0, groups + g)),
            pl.BlockSpec((1, s, hw), lambda bi, g, j: (bi, 0, 2 * groups + g)),
            pl.BlockSpec((1, hd), lambda bi, g, j: (0, 0)),
            pl.BlockSpec((1, hd), lambda bi, g, j: (0, 0)),
        ],
        out_specs=[
            pl.BlockSpec((1, MOBA_BLOCK, hw), lambda bi, g, j: (bi, j, g)),
            pl.BlockSpec((1, MOBA_BLOCK, hw), lambda bi, g, j: (bi, half - 1 - j, g)),
        ],
        out_shape=[jax.ShapeDtypeStruct((b, s // 2, A_WIDTH), BF16)] * 2,
        scratch_shapes=[
            pltpu.VMEM((MOBA_HEADS_PER_STEP, s, hd), BF16),
            pltpu.VMEM((MOBA_HEADS_PER_STEP, hd + MOBA_DENOM_ROWS, s), BF16),
            pltpu.VMEM((MOBA_HEADS_PER_STEP, nbp, hd), F32),
        ],
        compiler_params=pltpu.CompilerParams(
            dimension_semantics=("parallel", "parallel", "arbitrary"), vmem_limit_bytes=_vmem_limit(est)),
        name=name,
    )(proj, proj, proj, proj, q_gain.reshape(1, hd), k_gain.reshape(1, hd))
    return o_lo, o_hi


def _token_shift(p, prev_last, mu):
    rolled = pltpu.roll(p, 1, 0)
    row = lax.broadcasted_iota(jnp.int32, p.shape, 0)
    p_prev = jnp.where(row == 0, prev_last, rolled)
    return p + mu * (p_prev - p)


def _head_sums(x, ones_ref):
    tiles = [jnp.dot(x[:, c:c + V7X_LANES].astype(BF16), ones_ref[...], preferred_element_type=F32)
             for c in range(0, x.shape[1], V7X_LANES)]
    return jnp.concatenate(tiles, axis=1)


def _rwkv_prep_body(*refs, has_vres):
    if has_vres:
        (pm_ref, pz_ref, mum_ref, muz_ref, w0_ref, wl_ref, a0_ref, al_ref, gl_ref, kk_ref, ka_ref, rk_ref,
         ones_ref, v0_ref, vl_ref, vf_ref,
         r_ref, ld_ref, k_ref, v_ref, kn_ref, b_ref, bonus_ref, g_ref, cm_ref, cz_ref) = refs
    else:
        (pm_ref, pz_ref, mum_ref, muz_ref, w0_ref, wl_ref, a0_ref, al_ref, gl_ref, kk_ref, ka_ref, rk_ref,
         ones_ref,
         r_ref, ld_ref, k_ref, v_ref, kn_ref, b_ref, bonus_ref, g_ref, cm_ref, cz_ref) = refs
    tm = pm_ref.shape[1]
    bw = B_WIDTH

    @pl.when(pl.program_id(1) == 0)
    def _():
        cm_ref[...] = jnp.zeros_like(cm_ref)
        cz_ref[...] = jnp.zeros_like(cz_ref)

    pm = pm_ref[0].astype(F32)
    pz = pz_ref[0]
    last = V7X_SUBLANES - 1
    pms = _token_shift(pm, cm_ref[last:last + 1, :], mum_ref[...])
    pzs = _token_shift(pz, cz_ref[last:last + 1, :], muz_ref[...])
    cm_ref[...] = pm[tm - V7X_SUBLANES:, :]
    cz_ref[...] = pz[tm - V7X_SUBLANES:, :]

    r = pms[:, :bw]
    k = pms[:, bw:2 * bw]
    v = pms[:, 2 * bw:3 * bw]

    z_w = jnp.tanh(pzs[:, ZW_OFF:ZW_OFF + ZW_PAD]).astype(BF16)
    w = w0_ref[...] + jnp.dot(z_w, wl_ref[...], preferred_element_type=F32)
    w = -jax.nn.softplus(-w) - 0.5
    ld_ref[0] = -jnp.exp(w)

    z_a = pzs[:, ZA_OFF:ZA_OFF + ZA_PAD].astype(BF16)
    a = jax.nn.sigmoid(a0_ref[...] + jnp.dot(z_a, al_ref[...], preferred_element_type=F32))

    z_g = jax.nn.sigmoid(pzs[:, ZG_OFF:ZG_OFF + ZG_PAD]).astype(BF16)
    g_ref[0] = jnp.dot(z_g, gl_ref[...], preferred_element_type=F32).astype(g_ref.dtype)

    if has_vres:
        z_v = pzs[:, ZV_OFF:ZV_OFF + ZV_PAD].astype(BF16)
        mix = jax.nn.sigmoid(v0_ref[...] + jnp.dot(z_v, vl_ref[...], preferred_element_type=F32))
        v = v + (vf_ref[0].astype(F32) - v) * mix

    kk = k * kk_ref[...]
    kk = kk * jnp.minimum(lax.rsqrt(_head_sums(kk * kk, ones_ref)), 1e12)
    k = k * (1.0 + (a - 1.0) * ka_ref[...])
    r_ref[0] = r.astype(r_ref.dtype)
    k_ref[0] = k.astype(k_ref.dtype)
    v_ref[0] = v.astype(v_ref.dtype)
    kn_ref[0] = kk.astype(kn_ref.dtype)
    b_ref[0] = (kk * a).astype(b_ref.dtype)
    bonus_ref[0] = (_head_sums(r * k * rk_ref[...], ones_ref) * v).astype(bonus_ref.dtype)


def _pad_rows(w, rows):
    return jnp.pad(w, ((0, rows - w.shape[0]), (0, 0)))


def _head_pair_ones():
    head = jnp.arange(V7X_LANES) // B_HEAD_DIM
    return (head[:, None] == head[None, :]).astype(BF16)


def rwkv_prep(proj, pz, shift_mu, w0, w_lora, a0, a_lora, g_lora, k_k, k_a, r_k, v0, v_lora, v_first, name):
    b, s, main_blk = proj.shape
    bw = B_WIDTH
    has_vres = v_lora is not None
    tm = _pick(s, (256, 128))
    assert main_blk == 3 * bw

    mu_main = shift_mu[:3 * bw].reshape(1, 3 * bw)
    lora_mu = shift_mu[3 * bw:]
    mu_z = jnp.zeros((Z_WIDTH,), F32)
    mu_z = mu_z.at[ZW_OFF:ZW_OFF + LORA_W].set(lora_mu[:LORA_W])
    mu_z = mu_z.at[ZA_OFF:ZA_OFF + LORA_A].set(lora_mu[LORA_W:LORA_W + LORA_A])
    mu_z = mu_z.at[ZG_OFF:ZG_OFF + LORA_G].set(lora_mu[LORA_W + LORA_A:LORA_W + LORA_A + LORA_G])
    if has_vres:
        mu_z = mu_z.at[ZV_OFF:ZV_OFF + LORA_V].set(lora_mu[LORA_W + LORA_A + LORA_G:])
    mu_z = mu_z.reshape(1, Z_WIDTH)

    row = lambda t: t.reshape(1, bw)
    const = lambda shape: pl.BlockSpec(shape, lambda bi, i: (0, 0))
    tok = pl.BlockSpec((1, tm, bw), lambda bi, i: (bi, i, 0))
    args = [proj, pz, mu_main, mu_z, row(w0), _pad_rows(w_lora, ZW_PAD).astype(BF16),
            row(a0), _pad_rows(a_lora, ZA_PAD).astype(BF16), _pad_rows(g_lora, ZG_PAD).astype(BF16),
            row(k_k), row(k_a), row(r_k), _head_pair_ones()]
    in_specs = [
        pl.BlockSpec((1, tm, main_blk), lambda bi, i: (bi, i, 0)),
        pl.BlockSpec((1, tm, Z_WIDTH), lambda bi, i: (bi, i, 0)),
        const((1, main_blk)), const((1, Z_WIDTH)),
        const((1, bw)), const((ZW_PAD, bw)), const((1, bw)), const((ZA_PAD, bw)), const((ZG_PAD, bw)),
        const((1, bw)), const((1, bw)), const((1, bw)), const((V7X_LANES, V7X_LANES)),
    ]
    if has_vres:
        args += [row(v0), _pad_rows(v_lora, ZV_PAD).astype(BF16), v_first]
        in_specs += [const((1, bw)), const((ZV_PAD, bw)), tok]
    out_dtypes = [BF16, F32, BF16, BF16, BF16, BF16, BF16, BF16]
    n_out = len(out_dtypes)
    est = (2 * _nbytes((tm, main_blk + Z_WIDTH), F32) + (2 * n_out + 6) * _nbytes((tm, bw), F32)
           + 6 * _nbytes((tm, main_blk), F32) + 4 * _nbytes((ZG_PAD + 3 * ZW_PAD, bw), BF16))
    return pl.pallas_call(
        functools.partial(_rwkv_prep_body, has_vres=has_vres),
        grid=(b, s // tm),
        in_specs=in_specs,
        out_specs=[tok] * n_out,
        out_shape=[jax.ShapeDtypeStruct((b, s, bw), dt) for dt in out_dtypes],
        scratch_shapes=[pltpu.VMEM((V7X_SUBLANES, main_blk), F32), pltpu.VMEM((V7X_SUBLANES, Z_WIDTH), F32)],
        compiler_params=pltpu.CompilerParams(
            dimension_semantics=("parallel", "arbitrary"), vmem_limit_bytes=_vmem_limit(est)),
        name=name,
    )(*args)


def _cumsum_rows(x):
    row = lax.broadcasted_iota(jnp.int32, x.shape, 0)
    sh = 1
    while sh < x.shape[0]:
        x = x + jnp.where(row >= sh, pltpu.roll(x, sh, 0), 0.0)
        sh *= 2
    return x


def _stack_heads(x, m0, m1):
    return jnp.concatenate([x * m0, x * m1], axis=0)


def _nt_dot(a, b):
    return lax.dot_general(a.astype(BF16), b.astype(BF16), (((1,), (1,)), ((), ())), preferred_element_type=F32)


def _tn_dot(a, b):
    return lax.dot_general(a.astype(BF16), b.astype(BF16), (((0,), (0,)), ((), ())), preferred_element_type=F32)


def _nn_dot(a, b):
    return jnp.dot(a.astype(BF16), b.astype(BF16), preferred_element_type=F32)


def _rwkv_chunk(streams, hts, masks):
    pairs = range(len(streams))
    c = streams[0][0].shape[0]
    n2 = 2 * c
    m0, m1, strict, incl, eye, levels = masks
    ar, bk, v_st, p_last = [], [], [], []
    for r, ld, k, v, kk, b in streams:
        cs = _cumsum_rows(ld)
        p = jnp.exp(cs)
        p_inv = jnp.exp(-cs)
        a_st = _stack_heads(-kk * jnp.exp(cs - ld), m0, m1)
        r_st = _stack_heads(r * p, m0, m1)
        b_st = _stack_heads(b * p_inv, m0, m1)
        k_st = _stack_heads(k * p_inv, m0, m1)
        ar.append(jnp.concatenate([a_st, r_st], axis=0).astype(BF16))
        bk.append(jnp.concatenate([b_st, k_st], axis=0).astype(BF16))
        v_st.append(_stack_heads(v, m0, m1))
        p_last.append(p[c - 1:c, :])

    q = [_nt_dot(ar[i], bk[i]) for i in pairs]
    l_ab = [q[i][:n2, :n2] * strict for i in pairs]
    t = [eye + l_ab[i] * levels[0] for i in pairs]
    for lvl in levels[1:]:
        tc = [_nn_dot(t[i], l_ab[i] * lvl) for i in pairs]
        t = [t[i] + _nn_dot(tc[i], t[i]) for i in pairs]

    ar_h = [_nt_dot(ar[i], hts[i]) for i in pairs]
    ak_v = [_nn_dot(q[i][:n2, n2:] * strict, v_st[i]) for i in pairs]
    u_st = [_nn_dot(t[i], ar_h[i][:n2] + ak_v[i]) for i in pairs]
    uv = [jnp.concatenate([u_st[i], v_st[i]], axis=0).astype(BF16) for i in pairs]
    y_st = [ar_h[i][n2:] + _nn_dot(q[i][n2:, :] * jnp.concatenate([incl, incl], axis=1), uv[i]) for i in pairs]
    ys = [y_st[i][:c] + y_st[i][c:] for i in pairs]
    ht_new = [(hts[i] + _tn_dot(uv[i], bk[i])) * p_last[i] for i in pairs]
    return ys, ht_new


def _rwkv_scan_body(r_ref, ld_ref, k_ref, v_ref, kk_ref, b_ref, bonus_ref, g_ref, gw_ref, gb_ref, ones_ref,
                    o_ref, ht_ref):
    c = r_ref.shape[1]
    n2 = 2 * c
    lanes = V7X_LANES

    @pl.when(pl.program_id(1) == 0)
    def _():
        ht_ref[...] = jnp.zeros_like(ht_ref)

    lane = lax.broadcasted_iota(jnp.int32, (1, lanes), 1)
    m0 = jnp.where(lane < B_HEAD_DIM, 1.0, 0.0)
    m1 = 1.0 - m0
    row = lax.broadcasted_iota(jnp.int32, (n2, n2), 0)
    col = lax.broadcasted_iota(jnp.int32, (n2, n2), 1)
    log2c = c.bit_length() - 1
    same = jnp.right_shift(row, log2c) == jnp.right_shift(col, log2c)
    strict = jnp.where(same & (row > col), 1.0, 0.0)
    incl = jnp.where(same & (row >= col), 1.0, 0.0)
    eye = jnp.where(row == col, 1.0, 0.0)
    levels = []
    for lh in range(log2c):
        blk = jnp.right_shift(row, lh + 1) == jnp.right_shift(col, lh + 1)
        lower_left = (jnp.bitwise_and(jnp.right_shift(row, lh), 1) == 1) & \
                     (jnp.bitwise_and(jnp.right_shift(col, lh), 1) == 0)
        levels.append(jnp.where(blk & lower_left, 1.0, 0.0))
    masks = (m0, m1, strict, incl, eye, levels)

    where = [(bi, slice(i * lanes, (i + 1) * lanes))
             for bi in range(r_ref.shape[0]) for i in range(r_ref.shape[2] // lanes)]
    streams = [tuple(ref[bi, :, cs].astype(F32) for ref in (r_ref, ld_ref, k_ref, v_ref, kk_ref, b_ref))
               for bi, cs in where]
    ys, ht_new = _rwkv_chunk(streams, [ht_ref[i] for i in range(len(where))], masks)
    for i in range(len(where)):
        ht_ref[i] = ht_new[i]
    inv_n = 1.0 / B_HEAD_DIM
    means = [jnp.dot(y.astype(BF16), ones_ref[...], preferred_element_type=F32) * inv_n for y in ys]
    ds = [y - mean for y, mean in zip(ys, means)]
    vrs = [jnp.dot((d * d).astype(BF16), ones_ref[...], preferred_element_type=F32) * inv_n for d in ds]
    for (bi, cs), d, var in zip(where, ds, vrs):
        yn = d * lax.rsqrt(var + GN_EPS) * gw_ref[:, cs] + gb_ref[:, cs]
        o_ref[bi, :, cs] = ((yn + bonus_ref[bi, :, cs].astype(F32)) * g_ref[bi, :, cs].astype(F32)
                            ).astype(o_ref.dtype)


def rwkv_scan(r, ld, k, v, kk, b, bonus, g, gn_w, gn_b, name):
    bsz, s, bw = r.shape
    c = RWKV_CHUNK
    nb = _pick(bsz, (RWKV_SEQS_PER_STEP, 1))
    assert s % c == 0 and c & (c - 1) == 0 and 2 * B_HEAD_DIM == V7X_LANES
    tok = pl.BlockSpec((nb, c, bw), lambda bi, i: (bi, i, 0))
    par = pl.BlockSpec((1, bw), lambda bi, i: (0, 0))
    chains = nb * (bw // V7X_LANES)
    est = (18 * nb * _nbytes((c, bw), F32) + chains * _nbytes((V7X_LANES, V7X_LANES), F32)
           + 8 * chains * _nbytes((4 * c, 4 * c), F32))
    return pl.pallas_call(
        _rwkv_scan_body,
        grid=(bsz // nb, s // c),
        in_specs=[tok] * 8 + [par] * 2 + [pl.BlockSpec((V7X_LANES, V7X_LANES), lambda bi, i: (0, 0))],
        out_specs=tok,
        out_shape=jax.ShapeDtypeStruct((bsz, s, bw), BF16),
        scratch_shapes=[pltpu.VMEM((chains, V7X_LANES, V7X_LANES), F32)],
        compiler_params=pltpu.CompilerParams(
            dimension_semantics=("parallel", "arbitrary"), vmem_limit_bytes=_vmem_limit(est)),
        name=name,
    )(r, ld, k, v, kk, b, bonus, g, gn_w.reshape(1, bw), gn_b.reshape(1, bw), _head_pair_ones())


def _lora_proj_weight(w_in, has_vres):
    o = 3 * A_WIDTH + 3 * B_WIDTH
    d = w_in.shape[0]
    wz = jnp.zeros((d, Z_WIDTH), w_in.dtype)
    wz = wz.at[:, ZW_OFF:ZW_OFF + LORA_W].set(w_in[:, o:o + LORA_W])
    o += LORA_W
    wz = wz.at[:, ZA_OFF:ZA_OFF + LORA_A].set(w_in[:, o:o + LORA_A])
    o += LORA_A
    wz = wz.at[:, ZG_OFF:ZG_OFF + LORA_G].set(w_in[:, o:o + LORA_G])
    o += LORA_G
    if has_vres:
        wz = wz.at[:, ZV_OFF:ZV_OFF + LORA_V].set(w_in[:, o:o + LORA_V])
    return wz


def moba_rwkv_layer(x, b, s, li, norm_mix, w_in, q_gain, k_gain, shift_mu, w0, w_lora, a0, a_lora, g_lora,
                    k_k, k_a, r_k, gn_w, gn_b, w_out, v0, v_lora, v_first):
    has_vres = v_lora is not None
    tag = "l%d_" % li
    n_main = 3 * A_WIDTH + 3 * B_WIDTH
    proj_a, proj_b, pz = norm_matmul_split(
        x, norm_mix, w_in.astype(BF16), n_main, 3 * A_WIDTH, _lora_proj_weight(w_in, has_vres).astype(BF16),
        F32, BF16, F32, tag + "in_proj")
    pz = pz.reshape(b, s, Z_WIDTH)

    ya_lo, ya_hi = moba_attention(proj_a.reshape(b, s, 3 * A_WIDTH), q_gain, k_gain, tag + "moba")

    r, ld, k, v, kk, kb, bonus, g = rwkv_prep(proj_b.reshape(b, s, 3 * B_WIDTH), pz, shift_mu, w0, w_lora, a0,
                                              a_lora, g_lora, k_k, k_a, r_k.reshape(-1), v0, v_lora, v_first,
                                              tag + "rwkv_prep")
    if not has_vres:
        v_first = v
    y_b = rwkv_scan(r, ld, k, v, kk, kb, bonus, g, gn_w, gn_b, tag + "rwkv_scan")

    x = mix_out_proj(ya_lo.reshape(b * s // 2, A_WIDTH), ya_hi.reshape(b * s // 2, A_WIDTH),
                     y_b.reshape(b * s, B_WIDTH), w_out.astype(BF16), x, s, tag + "out_proj")
    return x, v_first


def short_conv_layer(x, b, s, li, norm_mix, conv_in, conv_w, conv_out):
    tag = "l%d_" % li
    m = norm_matmul_conv(x, norm_mix, conv_in.astype(BF16), conv_w, s, 3, _short_conv_combine, tag + "conv_in")
    return matmul_residual(m, conv_out.astype(BF16), x, tag + "conv_out")


def conv_ffn(x, b, s, li, norm_ffn, ffn_up, ffn_conv, ffn_down):
    tag = "l%d_" % li
    m = norm_matmul_conv(x, norm_ffn, ffn_up.astype(BF16), ffn_conv, s, 2, _ffn_combine, tag + "ffn_up")
    return matmul_residual(m, ffn_down.astype(BF16), x, tag + "ffn_down")


def kernel(x, l0_norm_mix, l0_w_in, l0_q_gain, l0_k_gain, l0_shift_mu, l0_w0, l0_w_lora, l0_a0, l0_a_lora, l0_g_lora, l0_k_k, l0_k_a, l0_r_k, l0_gn_w, l0_gn_b, l0_w_out, l0_norm_ffn, l0_ffn_up, l0_ffn_conv, l0_ffn_down, l1_norm_mix, l1_conv_in, l1_conv_w, l1_conv_out, l1_norm_ffn, l1_ffn_up, l1_ffn_conv, l1_ffn_down, l2_norm_mix, l2_w_in, l2_q_gain, l2_k_gain, l2_shift_mu, l2_w0, l2_w_lora, l2_a0, l2_a_lora, l2_g_lora, l2_k_k, l2_k_a, l2_r_k, l2_gn_w, l2_gn_b, l2_w_out, l2_v0, l2_v_lora, l2_norm_ffn, l2_ffn_up, l2_ffn_conv, l2_ffn_down, l3_norm_mix, l3_conv_in, l3_conv_w, l3_conv_out, l3_norm_ffn, l3_ffn_up, l3_ffn_conv, l3_ffn_down):
    b, s, d = x.shape
    h = x.reshape(b * s, d)

    h, v_first = moba_rwkv_layer(
        h, b, s, 0, l0_norm_mix, l0_w_in, l0_q_gain, l0_k_gain, l0_shift_mu, l0_w0, l0_w_lora, l0_a0,
        l0_a_lora, l0_g_lora, l0_k_k, l0_k_a, l0_r_k, l0_gn_w, l0_gn_b, l0_w_out, None, None, None)
    h = conv_ffn(h, b, s, 0, l0_norm_ffn, l0_ffn_up, l0_ffn_conv, l0_ffn_down)

    h = short_conv_layer(h, b, s, 1, l1_norm_mix, l1_conv_in, l1_conv_w, l1_conv_out)
    h = conv_ffn(h, b, s, 1, l1_norm_ffn, l1_ffn_up, l1_ffn_conv, l1_ffn_down)

    h, _ = moba_rwkv_layer(
        h, b, s, 2, l2_norm_mix, l2_w_in, l2_q_gain, l2_k_gain, l2_shift_mu, l2_w0, l2_w_lora, l2_a0,
        l2_a_lora, l2_g_lora, l2_k_k, l2_k_a, l2_r_k, l2_gn_w, l2_gn_b, l2_w_out, l2_v0, l2_v_lora, v_first)
    h = conv_ffn(h, b, s, 2, l2_norm_ffn, l2_ffn_up, l2_ffn_conv, l2_ffn_down)

    h = short_conv_layer(h, b, s, 3, l3_norm_mix, l3_conv_in, l3_conv_w, l3_conv_out)
    h = conv_ffn(h, b, s, 3, l3_norm_ffn, l3_ffn_up, l3_ffn_conv, l3_ffn_down)
    return h.reshape(b, s, d)
```
